```python
import math
import jax, jax.numpy as jnp
from jax import lax
import numpy as np

D_MODEL = 4096
BATCH = 8
SEQ = 2048
DEPTH = 1
DEC_BATCH = 16
DEC_SEQ = 16
PAST_LEN = 4096

CHUNK = 64
N_RET_HEADS = 8
RET_DK = 256
RET_DV = 256
D_RET = N_RET_HEADS * RET_DK
D_RNN = 2048
N_RNN_BLOCKS = 8
RNN_BLOCK = D_RNN // N_RNN_BLOCKS
CONV_W = 4
RGLRU_C = 8.0
N_EXPERTS = 256
TOP_K = 8
N_GROUPS = 8
TOPK_GROUPS = 4
D_EXPERT = 512
D_SHARED = 512
ROUTED_SCALE = 2.5
EXPERT_BLOCK = 128
ROPE_BASE = 10000.0
LN_EPS = 1e-5
DN_ALPHA = (2.0 * DEPTH) ** 0.25
DN_BETA = (8.0 * DEPTH) ** -0.25
D_IN = 4 * D_RET + 2 * D_RNN + 2 * D_MODEL
IN_SPLITS = (D_RET, 2 * D_RET, 3 * D_RET, 4 * D_RET, 4 * D_RET + D_RNN, 4 * D_RET + 2 * D_RNN)

kernel_name = 'retnet_rglru_moe_streaming_encoder_step'


def layer_norm(x, w=None, b=None):
    xf = x.astype(jnp.float32)
    mu = xf.mean(-1, keepdims=True)
    var = jnp.square(xf - mu).mean(-1, keepdims=True)
    y = (xf - mu) * lax.rsqrt(var + LN_EPS)
    if w is not None:
        y = y * w + b
    return y.astype(x.dtype)


def rotary(x, pos):
    half = x.shape[-1] // 2
    inv = ROPE_BASE ** (-jnp.arange(half, dtype=jnp.float32) / half)
    ang = pos.astype(jnp.float32)[:, None] * inv[None, :]
    cos = jnp.cos(ang)[None, :, None, :]
    sin = jnp.sin(ang)[None, :, None, :]
    x1 = x[..., :half].astype(jnp.float32)
    x2 = x[..., half:].astype(jnp.float32)
    return jnp.concatenate([x1 * cos - x2 * sin, x1 * sin + x2 * cos], -1)


def retention(q, k, v, s0, n_chunks, chunk):
    B = q.shape[0]
    log_g = jnp.log1p(-jnp.exp2(-5.0 - jnp.arange(N_RET_HEADS, dtype=jnp.float32)))
    q, k, v = [t.astype(jnp.float32).reshape(B, n_chunks, chunk, N_RET_HEADS, -1) for t in (q, k, v)]
    i = jnp.arange(chunk, dtype=jnp.float32)
    intra = jnp.exp(jnp.abs(i[:, None] - i[None, :])[None] * log_g[:, None, None])
    scores = jnp.einsum('bcnhd,bcmhd->bchnm', q, k) * intra[None, None]
    o_intra = jnp.einsum('bchnm,bcmhe->bcnhe', scores, v)
    k_dec = k * jnp.exp((chunk - 1 - i)[:, None] * log_g[None, :])[None, None, :, :, None]
    kv = jnp.einsum('bcmhd,bcmhe->cbhde', k_dec, v)
    g_chunk = jnp.exp(chunk * log_g)[None, :, None, None]

    def step(s, kv_c):
        return g_chunk * s + kv_c, s

    s_final, s_prev = lax.scan(step, s0.astype(jnp.float32), kv)
    q_dec = q * jnp.exp((i + 1.0)[:, None] * log_g[None, :])[None, None, :, :, None]
    o_cross = jnp.einsum('bcnhd,cbhde->bcnhe', q_dec, s_prev)
    o = (o_intra + o_cross).reshape(B, n_chunks * chunk, N_RET_HEADS, RET_DV)
    return o, s_final


def head_group_norm(o, w):
    B, T = o.shape[:2]
    mu = o.mean(-1, keepdims=True)
    var = jnp.square(o - mu).mean(-1, keepdims=True)
    return ((o - mu) * lax.rsqrt(var + LN_EPS)).reshape(B, T, D_RET) * w


def causal_conv(x, buf, w, b):
    T = x.shape[1]
    xp = jnp.concatenate([buf.astype(x.dtype), x], 1)
    y = b + xp[:, 0:T] * w[0]
    for j in range(1, CONV_W):
        y = y + xp[:, j:j + T] * w[j]
    return y, xp[:, -(CONV_W - 1):]


def rg_lru(x, h0, w_a, b_a, w_x, b_x, lam):
    B, T, _ = x.shape
    xf = x.astype(jnp.float32)
    xb = xf.reshape(B, T, N_RNN_BLOCKS, RNN_BLOCK)
    r = jax.nn.sigmoid(jnp.einsum('btnd,nde->btne', xb, w_a).reshape(B, T, D_RNN) + b_a)
    ig = jax.nn.sigmoid(jnp.einsum('btnd,nde->btne', xb, w_x).reshape(B, T, D_RNN) + b_x)
    log_a = -RGLRU_C * r * jax.nn.softplus(-lam)
    a = jnp.exp(log_a)
    u = jnp.sqrt(-jnp.expm1(2.0 * log_a)) * (ig * xf)
    u = u.at[:, 0].add(a[:, 0] * h0.astype(jnp.float32))

    def comb(e1, e2):
        a1, b1 = e1
        a2, b2 = e2
        return a1 * a2, a2 * b1 + b2

    _, h = lax.associative_scan(comb, (a, u), axis=1)
    return h, h[:, -1]


def swiglu(x, w1, w3, w2):
    return (jax.nn.silu(x @ w1) * (x @ w3)) @ w2


def route(h, w_router, router_bias):
    T = h.shape[0]
    s = jax.nn.sigmoid(h.astype(jnp.float32) @ w_router.astype(jnp.float32))
    sb = s + router_bias
    gscore = lax.top_k(sb.reshape(T, N_GROUPS, N_EXPERTS // N_GROUPS), 2)[0].sum(-1)
    _, gidx = lax.top_k(gscore, TOPK_GROUPS)
    gmask = jax.nn.one_hot(gidx, N_GROUPS, dtype=jnp.float32).sum(1)
    sel = jnp.where(jnp.repeat(gmask, N_EXPERTS // N_GROUPS, axis=-1) > 0, sb, -jnp.inf)
    _, idx = lax.top_k(sel, TOP_K)
    w = jnp.take_along_axis(s, idx, -1)
    w = w / w.sum(-1, keepdims=True) * ROUTED_SCALE
    return idx, w


def routed_experts(xt, idx, wts, w1, w3, w2):
    T, D = xt.shape
    A = T * TOP_K
    n_blocks = -(-A // EXPERT_BLOCK) + N_EXPERTS
    e_flat = idx.reshape(A)
    order = jnp.argsort(e_flat, stable=True)
    e_sorted = e_flat[order]
    counts = jnp.zeros((N_EXPERTS,), jnp.int32).at[e_flat].add(1)
    padded = (counts + EXPERT_BLOCK - 1) // EXPERT_BLOCK * EXPERT_BLOCK
    start = jnp.cumsum(counts) - counts
    pend = jnp.cumsum(padded)
    pstart = pend - padded
    dest = pstart[e_sorted] + jnp.arange(A, dtype=jnp.int32) - start[e_sorted]
    n_slots = n_blocks * EXPERT_BLOCK
    slot_tok = jnp.full((n_slots,), T, jnp.int32).at[dest].set((order // TOP_K).astype(jnp.int32))
    slot_w = jnp.zeros((n_slots,), wts.dtype).at[dest].set(wts.reshape(A)[order])
    blk_e = jnp.minimum(jnp.searchsorted(pend, jnp.arange(n_blocks, dtype=jnp.int32) * EXPERT_BLOCK, side='right'), N_EXPERTS - 1)
    x_pad = jnp.concatenate([xt, jnp.zeros((1, D), xt.dtype)], 0)

    def step(acc, blk):
        tok, wb, e = blk
        yb = swiglu(x_pad[tok], w1[e], w3[e], w2[e]) * wb[:, None]
        return acc.at[tok].add(yb.astype(acc.dtype)), None

    acc, _ = lax.scan(step, jnp.zeros((T + 1, D), xt.dtype),
                      (slot_tok.reshape(n_blocks, EXPERT_BLOCK), slot_w.reshape(n_blocks, EXPERT_BLOCK), blk_e))
    return acc[:T]


def trunk_layer(x, c, pos, s_ret, h_rnn, conv_buf, w_ada, b_ada, w_in, ret_gn_w, conv_w, conv_b,
                w_a, b_a, w_x, b_x, rg_lambda, w_br_ret, w_br_rnn, w_o, ln1_w, ln1_b,
                w_router, router_bias, w1, w3, w2, ws1, ws3, ws2, ln2_w, ln2_b):
    B, T, _ = x.shape
    mod = jax.nn.silu(c) @ w_ada + b_ada
    sh1, sc1, g1, sh2, sc2, g2 = jnp.split(mod[:, None, :], 6, axis=-1)
    h = layer_norm(x) * (1.0 + sc1) + sh1
    proj = h @ w_in
    q, k, v, g_ret, x_rnn, g_rnn, gate_logits = jnp.split(proj, IN_SPLITS, axis=-1)
    q = rotary(q.reshape(B, T, N_RET_HEADS, RET_DK), pos) * (RET_DK ** -0.5)
    k = rotary(k.reshape(B, T, N_RET_HEADS, RET_DK), pos)
    v = v.reshape(B, T, N_RET_HEADS, RET_DV)
    chunk = min(T, CHUNK)
    o, s_ret_new = retention(q, k, v, s_ret, T // chunk, chunk)
    o = head_group_norm(o, ret_gn_w).astype(x.dtype)
    y_ret = (jax.nn.silu(g_ret) * o) @ w_br_ret
    xc, conv_new = causal_conv(x_rnn, conv_buf, conv_w, conv_b)
    hr, h_new = rg_lru(xc, h_rnn, w_a, b_a, w_x, b_x, rg_lambda)
    y_rnn = (jax.nn.gelu(g_rnn) * hr.astype(x.dtype)) @ w_br_rnn
    gm = jax.nn.sigmoid(gate_logits).reshape(B, T, 2, D_MODEL)
    mix = (gm[:, :, 0] * y_ret + gm[:, :, 1] * y_rnn) @ w_o
    x = layer_norm(DN_ALPHA * x + g1 * mix, ln1_w, ln1_b)
    h2 = (layer_norm(x) * (1.0 + sc2) + sh2).reshape(B * T, D_MODEL)
    idx, wts = route(h2, w_router, router_bias)
    ffn = routed_experts(h2, idx, wts, w1, w3, w2) + swiglu(h2, ws1, ws3, ws2)
    x = layer_norm(DN_ALPHA * x + g2 * ffn.reshape(B, T, D_MODEL), ln2_w, ln2_b)
    return x, s_ret_new, h_new, conv_new


def setup_inputs(seed: int = 0) -> dict:
    key = jax.random.key(seed)
    ks = jax.random.split(key, 40)
    f32 = jnp.float32
    L = DEPTH

    def nrm(k, shape, scale):
        return jax.random.normal(k, shape, f32) * scale

    u = jax.random.uniform(ks[14], (L, D_RNN), f32, minval=0.9, maxval=0.999)
    a0 = u ** (1.0 / RGLRU_C)
    return {
        'x_prompt': nrm(ks[0], (BATCH, SEQ, D_MODEL), 1.0),
        'x_sample': nrm(ks[1], (DEC_BATCH, DEC_SEQ, D_MODEL), 1.0),
        'state_ret': nrm(ks[2], (L, DEC_BATCH, N_RET_HEADS, RET_DK, RET_DV), 1.0),
        'state_rglru': nrm(ks[3], (L, DEC_BATCH, D_RNN), 0.5),
        'state_conv': nrm(ks[4], (L, DEC_BATCH, CONV_W - 1, D_RNN), 1.0),
        'c_prompt': nrm(ks[5], (BATCH, D_MODEL), 1.0),
        'c_sample': nrm(ks[6], (DEC_BATCH, D_MODEL), 1.0),
        'w_ada': nrm(ks[7], (L, D_MODEL, 6 * D_MODEL), 0.5 * D_MODEL ** -0.5),
        'b_ada': nrm(ks[8], (L, 6 * D_MODEL), 0.02),
        'w_in': nrm(ks[9], (L, D_MODEL, D_IN), D_MODEL ** -0.5),
        'ret_gn_w': 1.0 + nrm(ks[10], (L, D_RET), 0.02),
        'conv_w': nrm(ks[11], (L, CONV_W, D_RNN), 0.5),
        'conv_b': nrm(ks[12], (L, D_RNN), 0.02),
        'w_a': nrm(ks[13], (L, N_RNN_BLOCKS, RNN_BLOCK, RNN_BLOCK), RNN_BLOCK ** -0.5),
        'b_a': nrm(ks[15], (L, D_RNN), 0.02),
        'w_x': nrm(ks[16], (L, N_RNN_BLOCKS, RNN_BLOCK, RNN_BLOCK), RNN_BLOCK ** -0.5),
        'b_x': nrm(ks[17], (L, D_RNN), 0.02),
        'rg_lambda': jnp.log(a0) - jnp.log1p(-a0),
        'w_br_ret': nrm(ks[18], (L, D_RET, D_MODEL), DN_BETA * D_RET ** -0.5),
        'w_br_rnn': nrm(ks[19], (L, D_RNN, D_MODEL), DN_BETA * D_RNN ** -0.5),
        'w_o': nrm(ks[20], (L, D_MODEL, D_MODEL), DN_BETA * D_MODEL ** -0.5),
        'ln1_w': 1.0 + nrm(ks[21], (L, D_MODEL), 0.02),
        'ln1_b': nrm(ks[22], (L, D_MODEL), 0.02),
        'w_router': nrm(ks[23], (L, D_MODEL, N_EXPERTS), D_MODEL ** -0.5),
        'router_bias': nrm(ks[24], (L, N_EXPERTS), 0.01),
        'w1': nrm(ks[25], (L, N_EXPERTS, D_MODEL, D_EXPERT), D_MODEL ** -0.5),
        'w3': nrm(ks[26], (L, N_EXPERTS, D_MODEL, D_EXPERT), D_MODEL ** -0.5),
        'w2': nrm(ks[27], (L, N_EXPERTS, D_EXPERT, D_MODEL), DN_BETA * D_EXPERT ** -0.5),
        'ws1': nrm(ks[28], (L, D_MODEL, D_SHARED), D_MODEL ** -0.5),
        'ws3': nrm(ks[29], (L, D_MODEL, D_SHARED), D_MODEL ** -0.5),
        'ws2': nrm(ks[30], (L, D_SHARED, D_MODEL), DN_BETA * D_SHARED ** -0.5),
        'ln2_w': 1.0 + nrm(ks[31], (L, D_MODEL), 0.02),
        'ln2_b': nrm(ks[32], (L, D_MODEL), 0.02),
    }


def reference(x_prompt, x_sample, state_ret, state_rglru, state_conv, c_prompt, c_sample,
              w_ada, b_ada, w_in, ret_gn_w, conv_w, conv_b, w_a, b_a, w_x, b_x, rg_lambda,
              w_br_ret, w_br_rnn, w_o, ln1_w, ln1_b, w_router, router_bias, w1, w3, w2,
              ws1, ws3, ws2, ln2_w, ln2_b):
    B, T = x_prompt.shape[0], x_prompt.shape[1]
    Td = x_sample.shape[1]
    pos_p = jnp.arange(T, dtype=jnp.int32)
    pos_s = PAST_LEN + jnp.arange(Td, dtype=jnp.int32)
    zero_ret = jnp.zeros((B, N_RET_HEADS, RET_DK, RET_DV), jnp.float32)
    zero_h = jnp.zeros((B, D_RNN), x_prompt.dtype)
    zero_conv = jnp.zeros((B, CONV_W - 1, D_RNN), x_prompt.dtype)
    yp, ys = x_prompt, x_sample
    rp, hp, cp, rs, hs, cs = [], [], [], [], [], []
    for l in range(DEPTH):
        lw = (w_ada[l], b_ada[l], w_in[l], ret_gn_w[l], conv_w[l], conv_b[l], w_a[l], b_a[l],
              w_x[l], b_x[l], rg_lambda[l], w_br_ret[l], w_br_rnn[l], w_o[l], ln1_w[l], ln1_b[l],
              w_router[l], router_bias[l], w1[l], w3[l], w2[l], ws1[l], ws3[l], ws2[l],
              ln2_w[l], ln2_b[l])
        yp, r1, h1, c1 = trunk_layer(yp, c_prompt, pos_p, zero_ret, zero_h, zero_conv, *lw)
        ys, r2, h2, c2 = trunk_layer(ys, c_sample, pos_s, state_ret[l], state_rglru[l], state_conv[l], *lw)
        rp.append(r1); hp.append(h1); cp.append(c1)
        rs.append(r2); hs.append(h2); cs.append(c2)
    return (yp, ys, jnp.stack(rp), jnp.stack(hp), jnp.stack(cp), jnp.stack(rs), jnp.stack(hs), jnp.stack(cs))
```

```python
import functools
import math

import jax
import jax.numpy as jnp
from jax import lax
from jax.experimental import pallas as pl
from jax.experimental.pallas import tpu as pltpu

CHUNK = 64
TOP_K = 8
N_GROUPS = 8
TOPK_GROUPS = 4
ROUTED_SCALE = 2.5
RGLRU_C = 8.0
PAST_LEN = 4096
ROPE_BASE = 10000.0
LN_EPS = 1e-5

F32 = jnp.float32
BF16 = jnp.bfloat16

V7X_VMEM_LIMIT_BYTES = 56 * 1024 * 1024
LANES = 128
EXPERT_ROWS = 128
COMBINE_ROWS = 64
CONV_PAD = 8


def _cparams(*sem):
    return pltpu.CompilerParams(dimension_semantics=sem, vmem_limit_bytes=V7X_VMEM_LIMIT_BYTES)


def _pick(n, cands):
    for c in cands:
        if n % c == 0:
            return c
    return n


def _ln(x):
    mu = jnp.mean(x, axis=-1, keepdims=True)
    xc = x - mu
    var = jnp.mean(xc * xc, axis=-1, keepdims=True)
    return xc * lax.rsqrt(var + LN_EPS)


def _sigmoid(x):
    return jax.nn.sigmoid(x)


def _dot(a, b):
    return jnp.dot(a, b, preferred_element_type=F32)


def _ada_kernel(c_ref, w_ref, b_ref, o_ref):
    c = c_ref[...]
    s = (c * _sigmoid(c)).astype(BF16)
    o_ref[...] = _dot(s, w_ref[...].astype(BF16)) + b_ref[...]


def ada_mod(c, w, b):
    R, D = c.shape
    N = w.shape[1]
    tn = _pick(N, (512, 256, 128))
    return pl.pallas_call(
        _ada_kernel,
        grid=(N // tn,),
        in_specs=[pl.BlockSpec((R, D), lambda j: (0, 0)),
                  pl.BlockSpec((D, tn), lambda j: (0, j)),
                  pl.BlockSpec((1, tn), lambda j: (0, j))],
        out_specs=pl.BlockSpec((R, tn), lambda j: (0, j)),
        out_shape=jax.ShapeDtypeStruct((R, N), F32),
        compiler_params=_cparams("arbitrary"),
        name="ada_mod",
    )(c, w, b)


def _mod_spec(arr, tm):
    D = arr.shape[-1]
    if arr.shape[1] == 1:
        return pl.BlockSpec((None, 1, D), lambda g, i: (g, 0, 0))
    return pl.BlockSpec((None, tm, D), lambda g, i: (g, i, 0))


def _lnmod_kernel(x_ref, sc_ref, sh_ref, o_ref):
    y = _ln(x_ref[...]) * (1.0 + sc_ref[...]) + sh_ref[...]
    o_ref[...] = y.astype(o_ref.dtype)


def ln_mod(x, sc, sh):
    G, R, D = x.shape
    tm = _pick(R, (512, 256, 128, 64, 32, 16, 8))
    row = pl.BlockSpec((None, tm, D), lambda g, i: (g, i, 0))
    return pl.pallas_call(
        _lnmod_kernel,
        grid=(G, R // tm),
        in_specs=[row, _mod_spec(sc, tm), _mod_spec(sh, tm)],
        out_specs=row,
        out_shape=jax.ShapeDtypeStruct((G, R, D), BF16),
        compiler_params=_cparams("parallel", "parallel"),
        name="ln_mod",
    )(x, sc, sh)


def _mm_kernel(a_ref, w_ref, o_ref):
    o_ref[...] = _dot(a_ref[...], w_ref[...]).astype(o_ref.dtype)


def matmul(a, w, out_dtype=F32):
    M, K = a.shape
    N = w.shape[1]
    tm = _pick(M, (1024, 512, 256, 128, 64, 32, 16, 8))
    tn = _pick(N, (1024, 512, 256, 128))
    return pl.pallas_call(
        _mm_kernel,
        grid=(M // tm, N // tn),
        in_specs=[pl.BlockSpec((tm, K), lambda i, j: (i, 0)),
                  pl.BlockSpec((K, tn), lambda i, j: (0, j))],
        out_specs=pl.BlockSpec((tm, tn), lambda i, j: (i, j)),
        out_shape=jax.ShapeDtypeStruct((M, N), out_dtype),
        compiler_params=_cparams("parallel", "parallel"),
        name="matmul",
    )(a, w)


def _rot(x, cos, sin):
    half = x.shape[-1] // 2
    x1 = x[:, :half]
    x2 = x[:, half:]
    return jnp.concatenate([x1 * cos - x2 * sin, x1 * sin + x2 * cos], axis=-1)


def _retention_kernel(q_ref, k_ref, v_ref, g_ref, cos_ref, sin_ref, intra_ref, kdec_ref, qdec_ref,
                      gch_ref, gnw_ref, s0_ref, o_ref, sfin_ref, s_scr, *, chunk, n_sub, q_scale):
    i = pl.program_id(2)

    @pl.when(i == 0)
    def _():
        s_scr[...] = s0_ref[...]

    intra = intra_ref[...]
    kdec = kdec_ref[...]
    qdec = qdec_ref[...]
    gch = gch_ref[...]
    gnw = gnw_ref[...]
    for c in range(n_sub):
        rows = pl.ds(c * chunk, chunk)
        cos = cos_ref[rows, :]
        sin = sin_ref[rows, :]
        q = _rot(q_ref[rows, :], cos, sin) * q_scale
        k = _rot(k_ref[rows, :], cos, sin)
        v = v_ref[rows, :].astype(BF16)
        s_prev = s_scr[...]
        scores = lax.dot_general(q.astype(BF16), k.astype(BF16), (((1,), (1,)), ((), ())),
                                 preferred_element_type=F32) * intra
        o = _dot(scores.astype(BF16), v) + _dot((q * qdec).astype(BF16), s_prev.astype(BF16))
        kv = lax.dot_general((k * kdec).astype(BF16), v, (((0,), (0,)), ((), ())),
                             preferred_element_type=F32)
        s_scr[...] = gch * s_prev + kv
        on = _ln(o) * gnw
        g = g_ref[rows, :]
        o_ref[rows, :] = (g * _sigmoid(g) * on).astype(o_ref.dtype)

    @pl.when(i == pl.num_programs(2) - 1)
    def _():
        sfin_ref[...] = s_scr[...]


def retention_mixer(proj, pos, s0, gn_w, n_heads, d_ret):
    B, T, _ = proj.shape
    dk = d_ret // n_heads
    dv = s0.shape[-1]
    chunk = min(T, CHUNK)
    n_chunks = T // chunk
    n_sub = _pick(n_chunks, (8, 4, 2, 1))
    tb = n_sub * chunk
    half = dk // 2
    inv = ROPE_BASE ** (-jnp.arange(half, dtype=F32) / half)
    ang = pos.astype(F32)[:, None] * inv[None, :]
    cos, sin = jnp.cos(ang), jnp.sin(ang)
    log_g = jnp.log1p(-jnp.exp2(-5.0 - jnp.arange(n_heads, dtype=F32)))
    ci = jnp.arange(chunk, dtype=F32)
    intra = jnp.exp(jnp.abs(ci[:, None] - ci[None, :])[None] * log_g[:, None, None])
    kdec = jnp.broadcast_to(jnp.exp((chunk - 1 - ci)[None, :] * log_g[:, None])[:, :, None],
                            (n_heads, chunk, dk))
    qdec = jnp.broadcast_to(jnp.exp((ci + 1.0)[None, :] * log_g[:, None])[:, :, None],
                            (n_heads, chunk, dk))
    gch = jnp.broadcast_to(jnp.exp(chunk * log_g)[:, None, None], (n_heads, 1, dv))

    def col(group):
        return pl.BlockSpec((None, tb, dk), lambda b, h, i: (b, i, group * n_heads + h))

    tab = pl.BlockSpec((tb, half), lambda b, h, i: (i, 0))
    per_head = lambda r, c: pl.BlockSpec((None, r, c), lambda b, h, i: (h, 0, 0))
    state = pl.BlockSpec((None, None, dk, dv), lambda b, h, i: (b, h, 0, 0))
    kern = functools.partial(_retention_kernel, chunk=chunk, n_sub=n_sub, q_scale=dk ** -0.5)
    return pl.pallas_call(
        kern,
        grid=(B, n_heads, T // tb),
        in_specs=[col(0), col(1), col(2), col(3), tab, tab,
                  per_head(chunk, chunk), per_head(chunk, dk), per_head(chunk, dk), per_head(1, dv),
                  pl.BlockSpec((1, dv), lambda b, h, i: (0, h)), state],
        out_specs=[pl.BlockSpec((None, tb, dv), lambda b, h, i: (b, i, h)), state],
        out_shape=[jax.ShapeDtypeStruct((B, T, n_heads * dv), BF16),
                   jax.ShapeDtypeStruct(s0.shape, F32)],
        scratch_shapes=[pltpu.VMEM((dk, dv), F32)],
        compiler_params=_cparams("parallel", "parallel", "arbitrary"),
        name="retention",
    )(proj, proj, proj, proj, cos, sin, intra, kdec, qdec, gch, gn_w, s0)


def _softplus(z):
    return jnp.maximum(z, 0.0) + jnp.log1p(jnp.exp(-jnp.abs(z)))


def _expm1(x):
    e = jnp.exp(x)
    em1 = e - 1.0
    return jnp.where(e == 1.0, x, jnp.where(em1 == -1.0, -1.0, em1 * x / jnp.log(e)))


def _gelu_tanh(x):
    return 0.5 * x * (1.0 + jnp.tanh(math.sqrt(2.0 / math.pi) * (x + 0.044715 * (x * x * x))))


def _rglru_kernel(x_ref, g_ref, cw_ref, cb_ref, wa_ref, ba_ref, wx_ref, bx_ref, lam_ref, h0_ref, c0_ref,
                  o_ref, hn_ref, cn_ref, xp_scr, a_scr, u_scr, h_scr, hc_scr, *, tb, conv_w, n_blocks):
    i = pl.program_id(1)
    hist = conv_w - 1
    lo = CONV_PAD - hist

    @pl.when(i == 0)
    def _():
        xp_scr[lo:CONV_PAD, :] = c0_ref[...]
        hc_scr[...] = h0_ref[...]

    @pl.when(i > 0)
    def _():
        xp_scr[lo:CONV_PAD, :] = xp_scr[tb + lo:tb + CONV_PAD, :]

    xp_scr[CONV_PAD:CONV_PAD + tb, :] = x_ref[...]
    xc = cb_ref[...] + xp_scr[lo:lo + tb, :] * cw_ref[0:1, :]
    for j in range(1, conv_w):
        xc = xc + xp_scr[lo + j:lo + j + tb, :] * cw_ref[j:j + 1, :]

    sp = RGLRU_C * _softplus(-lam_ref[...])
    bw = xc.shape[1] // n_blocks
    for n in range(n_blocks):
        cols = slice(n * bw, (n + 1) * bw)
        xb = xc[:, cols]
        xb16 = xb.astype(BF16)
        r = _sigmoid(_dot(xb16, wa_ref[n]) + ba_ref[:, cols])
        ig = _sigmoid(_dot(xb16, wx_ref[n]) + bx_ref[:, cols])
        log_a = -(r * sp[:, cols])
        a_scr[:, cols] = jnp.exp(log_a)
        u_scr[:, cols] = jnp.sqrt(-_expm1(2.0 * log_a)) * (ig * xb)

    def step(t, h):
        h = a_scr[pl.ds(t, 1), :] * h + u_scr[pl.ds(t, 1), :]
        h_scr[pl.ds(t, 1), :] = h
        return h

    h_last = lax.fori_loop(0, tb, step, hc_scr[...], unroll=8)
    hc_scr[...] = h_last
    o_ref[...] = (_gelu_tanh(g_ref[...]) * h_scr[...]).astype(o_ref.dtype)

    @pl.when(i == pl.num_programs(1) - 1)
    def _():
        hn_ref[...] = h_last
        cn_ref[...] = xp_scr[tb + lo:tb + CONV_PAD, :]


def rglru_mixer(proj, x_group, h0, conv0, conv_w, conv_b, w_a, b_a, w_x, b_x, lam):
    B, T, _ = proj.shape
    d_rnn = h0.shape[-1]
    cw = conv_w.shape[0]
    n_blocks = w_a.shape[0]
    tb = _pick(T, (256, 128, 64, 32, 16, 8))
    kern = functools.partial(_rglru_kernel, tb=tb, conv_w=cw, n_blocks=n_blocks)
    full = lambda *s: pl.BlockSpec(s, lambda b, i: (0,) * len(s))
    return pl.pallas_call(
        kern,
        grid=(B, T // tb),
        in_specs=[pl.BlockSpec((None, tb, d_rnn), lambda b, i: (b, i, x_group)),
                  pl.BlockSpec((None, tb, d_rnn), lambda b, i: (b, i, x_group + 1)),
                  full(cw, d_rnn), full(1, d_rnn),
                  full(*w_a.shape), full(1, d_rnn), full(*w_x.shape), full(1, d_rnn), full(1, d_rnn),
                  pl.BlockSpec((None, 1, d_rnn), lambda b, i: (b, 0, 0)),
                  pl.BlockSpec((None, cw - 1, d_rnn), lambda b, i: (b, 0, 0))],
        out_specs=[pl.BlockSpec((None, tb, d_rnn), lambda b, i: (b, i, 0)),
                   pl.BlockSpec((None, 1, d_rnn), lambda b, i: (b, 0, 0)),
                   pl.BlockSpec((None, cw - 1, d_rnn), lambda b, i: (b, 0, 0))],
        out_shape=[jax.ShapeDtypeStruct((B, T, d_rnn), BF16),
                   jax.ShapeDtypeStruct((B, 1, d_rnn), F32),
                   jax.ShapeDtypeStruct((B, cw - 1, d_rnn), F32)],
        scratch_shapes=[pltpu.VMEM((tb + CONV_PAD, d_rnn), F32), pltpu.VMEM((tb, d_rnn), F32),
                        pltpu.VMEM((tb, d_rnn), F32), pltpu.VMEM((tb, d_rnn), F32),
                        pltpu.VMEM((1, d_rnn), F32)],
        compiler_params=_cparams("parallel", "arbitrary"),
        name="rglru",
    )(proj, proj, conv_w, conv_b, w_a, b_a, w_x, b_x, lam, h0, conv0)


def _merge_kernel(pr_ref, pn_ref, wr_ref, wn_ref, g0_ref, g1_ref, o_ref):
    yr = _dot(pr_ref[...], wr_ref[...])
    yn = _dot(pn_ref[...], wn_ref[...])
    o_ref[...] = (_sigmoid(g0_ref[...]) * yr + _sigmoid(g1_ref[...]) * yn).astype(o_ref.dtype)


def merge_branches(pre_ret, pre_rnn, w_ret, w_rnn, proj, gate_off):
    M, kr = pre_ret.shape
    kn = pre_rnn.shape[1]
    D = w_ret.shape[1]
    tm = _pick(M, (1024, 512, 256, 128, 64, 32, 16, 8))
    tn = _pick(math.gcd(D, gate_off), (512, 256, 128))
    g0 = gate_off // tn
    g1 = (gate_off + D) // tn
    return pl.pallas_call(
        _merge_kernel,
        grid=(M // tm, D // tn),
        in_specs=[pl.BlockSpec((tm, kr), lambda i, j: (i, 0)),
                  pl.BlockSpec((tm, kn), lambda i, j: (i, 0)),
                  pl.BlockSpec((kr, tn), lambda i, j: (0, j)),
                  pl.BlockSpec((kn, tn), lambda i, j: (0, j)),
                  pl.BlockSpec((tm, tn), lambda i, j: (i, g0 + j)),
                  pl.BlockSpec((tm, tn), lambda i, j: (i, g1 + j))],
        out_specs=pl.BlockSpec((tm, tn), lambda i, j: (i, j)),
        out_shape=jax.ShapeDtypeStruct((M, D), BF16),
        compiler_params=_cparams("parallel", "parallel"),
        name="merge_branches",
    )(pre_ret, pre_rnn, w_ret, w_rnn, proj, proj)


def _post1_kernel(x_ref, mix_ref, g1_ref, sc_ref, sh_ref, w_ref, b_ref, x1_ref, h2_ref, *, alpha):
    x1 = _ln(alpha * x_ref[...] + g1_ref[...] * mix_ref[...]) * w_ref[...] + b_ref[...]
    x1_ref[...] = x1
    h2_ref[...] = _ln(x1) * (1.0 + sc_ref[...]) + sh_ref[...]


def post_mixer(x, mix, g1, sc2, sh2, ln_w, ln_b, alpha):
    G, R, D = x.shape
    tm = _pick(R, (256, 128, 64, 32, 16, 8))
    row = pl.BlockSpec((None, tm, D), lambda g, i: (g, i, 0))
    vec = pl.BlockSpec((1, D), lambda g, i: (0, 0))
    return pl.pallas_call(
        functools.partial(_post1_kernel, alpha=alpha),
        grid=(G, R // tm),
        in_specs=[row, row, _mod_spec(g1, tm), _mod_spec(sc2, tm), _mod_spec(sh2, tm), vec, vec],
        out_specs=[row, row],
        out_shape=[jax.ShapeDtypeStruct((G, R, D), F32), jax.ShapeDtypeStruct((G, R, D), F32)],
        compiler_params=_cparams("parallel", "parallel"),
        name="post_mixer",
    )(x, mix, g1, sc2, sh2, ln_w, ln_b)


def _router_kernel(h_ref, w_ref, b_ref, idx_ref, wt_ref, *, n_experts):
    tm = h_ref.shape[0]
    gsz = n_experts // N_GROUPS
    neg = -jnp.inf
    s = _sigmoid(_dot(h_ref[...].astype(BF16), w_ref[...]))
    sb = s + b_ref[...]
    lane = lax.broadcasted_iota(jnp.int32, (tm, n_experts), 1).astype(F32)
    grp = jnp.zeros((tm, n_experts), F32)
    for g in range(1, N_GROUPS):
        grp = grp + (lane >= float(g * gsz)).astype(F32)
    lane_o = lax.broadcasted_iota(jnp.int32, (tm, LANES), 1).astype(F32)

    def argmax_first(v, ids, big):
        m = jnp.max(v, axis=-1, keepdims=True)
        first = jnp.min(jnp.where(v == m, ids, float(big)), axis=-1, keepdims=True)
        return m, first

    gscore = jnp.full((tm, LANES), neg, F32)
    for g in range(N_GROUPS):
        vg = jnp.where(grp == g, sb, neg)
        m1, i1 = argmax_first(vg, lane, n_experts)
        m2 = jnp.max(jnp.where(lane == i1, neg, vg), axis=-1, keepdims=True)
        gscore = jnp.where(lane_o == g, m1 + m2, gscore)
    keep = jnp.zeros((tm, n_experts), jnp.bool_)
    for _ in range(TOPK_GROUPS):
        _, gi = argmax_first(gscore, lane_o, LANES)
        gscore = jnp.where(lane_o == gi, neg, gscore)
        keep = jnp.logical_or(keep, grp == gi)
    sel = jnp.where(keep, sb, neg)
    idx_o = jnp.zeros((tm, LANES), F32)
    wt_o = jnp.zeros((tm, LANES), F32)
    for k in range(TOP_K):
        _, ei = argmax_first(sel, lane, n_experts)
        hit = lane == ei
        sel = jnp.where(hit, neg, sel)
        wk = jnp.sum(jnp.where(hit, s, 0.0), axis=-1, keepdims=True)
        idx_o = jnp.where(lane_o == k, ei, idx_o)
        wt_o = jnp.where(lane_o == k, wk, wt_o)
    wt_o = wt_o / jnp.sum(wt_o, axis=-1, keepdims=True) * ROUTED_SCALE
    idx_ref[...] = idx_o.astype(jnp.int32)
    wt_ref[...] = wt_o


def router(h2, w_router, bias):
    M, D = h2.shape
    E = w_router.shape[1]
    tm = _pick(M, (256, 128, 64, 32, 16, 8))
    out = pl.BlockSpec((tm, LANES), lambda i: (i, 0))
    return pl.pallas_call(
        functools.partial(_router_kernel, n_experts=E),
        grid=(M // tm,),
        in_specs=[pl.BlockSpec((tm, D), lambda i: (i, 0)),
                  pl.BlockSpec((D, E), lambda i: (0, 0)),
                  pl.BlockSpec((1, E), lambda i: (0, 0))],
        out_specs=[out, out],
        out_shape=[jax.ShapeDtypeStruct((M, LANES), jnp.int32), jax.ShapeDtypeStruct((M, LANES), F32)],
        compiler_params=_cparams("parallel"),
        name="router",
    )(h2, w_router, bias)


def _row_copy(src_hbm, row, dst, r, sem):
    return pltpu.make_async_copy(src_hbm.at[pl.ds(row, 1)], dst.at[pl.ds(r, 1)], sem)


def _experts_kernel(blk_e_ref, tok_hbm, h_hbm, w1_ref, w3_ref, w2_ref, y_ref,
                    xbuf, tok_smem, gsem, isem, *, bm):
    i = pl.program_id(0)
    nb = pl.num_programs(0)
    slot = i % 2
    nxt = 1 - slot

    def idx_copy(blk, s):
        return pltpu.make_async_copy(tok_hbm.at[blk], tok_smem.at[s], isem.at[s])

    def issue_rows(s):
        def body(r, c):
            _row_copy(h_hbm, tok_smem[s, r], xbuf.at[s], r, gsem.at[s]).start()
            return c
        lax.fori_loop(0, bm, body, 0, unroll=8)

    def wait_rows(s):
        def body(r, c):
            _row_copy(h_hbm, 0, xbuf.at[s], r, gsem.at[s]).wait()
            return c
        lax.fori_loop(0, bm, body, 0, unroll=8)

    @pl.when(i == 0)
    def _():
        idx_copy(0, 0).start()
        idx_copy(0, 0).wait()
        issue_rows(0)

        @pl.when(nb > 1)
        def _():
            idx_copy(1, 1).start()

    @pl.when(i + 1 < nb)
    def _():
        idx_copy(i + 1, nxt).wait()
        issue_rows(nxt)

    wait_rows(slot)

    @pl.when(i + 2 < nb)
    def _():
        idx_copy(i + 2, slot).start()

    x = xbuf[slot].astype(BF16)
    h1 = _dot(x, w1_ref[...])
    h3 = _dot(x, w3_ref[...])
    hh = (h1 * _sigmoid(h1) * h3).astype(BF16)
    y_ref[...] = _dot(hh, w2_ref[...])


def routed_expert_blocks(h2, slot_tok, blk_e, w1, w3, w2):
    M, D = h2.shape
    nb, bm = slot_tok.shape
    F = w1.shape[-1]
    grid_spec = pltpu.PrefetchScalarGridSpec(
        num_scalar_prefetch=1,
        grid=(nb,),
        in_specs=[pl.BlockSpec(memory_space=pl.ANY),
                  pl.BlockSpec(memory_space=pl.ANY),
                  pl.BlockSpec((None, D, F), lambda i, be: (be[i], 0, 0)),
                  pl.BlockSpec((None, D, F), lambda i, be: (be[i], 0, 0)),
                  pl.BlockSpec((None, F, D), lambda i, be: (be[i], 0, 0))],
        out_specs=pl.BlockSpec((bm, D), lambda i, be: (i, 0)),
        scratch_shapes=[pltpu.VMEM((2, bm, D), F32), pltpu.SMEM((2, bm), jnp.int32),
                        pltpu.SemaphoreType.DMA((2,)), pltpu.SemaphoreType.DMA((2,))],
    )
    return pl.pallas_call(
        functools.partial(_experts_kernel, bm=bm),
        grid_spec=grid_spec,
        out_shape=jax.ShapeDtypeStruct((nb * bm, D), F32),
        compiler_params=_cparams("arbitrary"),
        name="routed_experts",
    )(blk_e, slot_tok, h2, w1, w3, w2)


def _final_kernel(pos_hbm, y_hbm, x1_ref, h2_ref, wt_ref, g2_ref, ws1_ref, ws3_ref, ws2_ref, w_ref, b_ref,
                  o_ref, ybuf, pos_smem, gsem, isem, *, tm, alpha, tile_off):
    g = pl.program_id(0)
    i = pl.program_id(1)
    nt = pl.num_programs(1)
    step = g * nt + i
    nsteps = pl.num_programs(0) * nt
    slot = step % 2
    nxt = 1 - slot
    n_rows = tm * TOP_K

    def idx_copy(st, s):
        return pltpu.make_async_copy(pos_hbm.at[tile_off + st], pos_smem.at[s], isem.at[s])

    def issue_rows(s):
        def body(r, c):
            _row_copy(y_hbm, pos_smem[s, r], ybuf.at[s], r, gsem.at[s]).start()
            return c
        lax.fori_loop(0, n_rows, body, 0, unroll=8)

    def wait_rows(s):
        def body(r, c):
            _row_copy(y_hbm, 0, ybuf.at[s], r, gsem.at[s]).wait()
            return c
        lax.fori_loop(0, n_rows, body, 0, unroll=8)

    @pl.when(step == 0)
    def _():
        idx_copy(0, 0).start()
        idx_copy(0, 0).wait()
        issue_rows(0)

        @pl.when(nsteps > 1)
        def _():
            idx_copy(1, 1).start()

    @pl.when(step + 1 < nsteps)
    def _():
        idx_copy(step + 1, nxt).wait()
        issue_rows(nxt)

    h = h2_ref[...].astype(BF16)
    a1 = _dot(h, ws1_ref[...])
    a3 = _dot(h, ws3_ref[...])
    ffn = _dot((a1 * _sigmoid(a1) * a3).astype(BF16), ws2_ref[...])

    wait_rows(slot)

    @pl.when(step + 2 < nsteps)
    def _():
        idx_copy(step + 2, slot).start()

    wt = wt_ref[...]
    for k in range(TOP_K):
        ffn = ffn + wt[:, k:k + 1] * ybuf[slot, k * tm:(k + 1) * tm, :]
    o_ref[...] = _ln(alpha * x1_ref[...] + g2_ref[...] * ffn) * w_ref[...] + b_ref[...]


def combine_and_finish(x1, h2, wts, pos_tiles, tile_off, y, g2, ws1, ws3, ws2, ln_w, ln_b, alpha):
    G, R, D = x1.shape
    tm = min(R, COMBINE_ROWS)
    F = ws1.shape[1]
    row = pl.BlockSpec((None, tm, D), lambda g, i: (g, i, 0))
    vec = pl.BlockSpec((1, D), lambda g, i: (0, 0))
    const = lambda r, c: pl.BlockSpec((r, c), lambda g, i: (0, 0))
    kern = functools.partial(_final_kernel, tm=tm, alpha=alpha, tile_off=tile_off)
    return pl.pallas_call(
        kern,
        grid=(G, R // tm),
        in_specs=[pl.BlockSpec(memory_space=pl.ANY), pl.BlockSpec(memory_space=pl.ANY),
                  row, row, pl.BlockSpec((None, tm, LANES), lambda g, i: (g, i, 0)), _mod_spec(g2, tm),
                  const(D, F), const(D, F), const(F, D), vec, vec],
        out_specs=row,
        out_shape=jax.ShapeDtypeStruct((G, R, D), F32),
        scratch_shapes=[pltpu.VMEM((2, tm * TOP_K, D), F32), pltpu.SMEM((2, tm * TOP_K), jnp.int32),
                        pltpu.SemaphoreType.DMA((2,)), pltpu.SemaphoreType.DMA((2,))],
        compiler_params=_cparams("arbitrary", "arbitrary"),
        name="combine_finish",
    )(pos_tiles, y, x1, h2, wts, g2, ws1, ws3, ws2, ln_w, ln_b)


def _dispatch_tables(idx, n_experts, bm):
    M = idx.shape[0]
    A = M * TOP_K
    nb = -(-A // bm) + n_experts
    e_flat = idx.reshape(A)
    a_ids = jnp.arange(A, dtype=jnp.int32)
    e_sorted, order = lax.sort((e_flat, a_ids), num_keys=1, is_stable=True)
    start = jnp.searchsorted(e_sorted, jnp.arange(n_experts + 1, dtype=jnp.int32), side="left").astype(jnp.int32)
    counts = start[1:] - start[:-1]
    start = start[:-1]
    padded = (counts + bm - 1) // bm * bm
    pend = jnp.cumsum(padded)
    pstart = pend - padded
    dest = pstart[e_sorted] + a_ids - start[e_sorted]
    _, pos = lax.sort((order, dest), num_keys=1)
    blk_e = jnp.minimum(jnp.searchsorted(pend, jnp.arange(nb, dtype=jnp.int32) * bm, side="right"),
                        n_experts - 1).astype(jnp.int32)
    slots = jnp.arange(nb * bm, dtype=jnp.int32)
    se = jnp.repeat(blk_e, bm)
    r = slots - pstart[se]
    valid = (r >= 0) & (r < counts[se])
    src = jnp.clip(start[se] + r, 0, A - 1)
    slot_tok = jnp.where(valid, order[src] // TOP_K, 0).astype(jnp.int32)
    return slot_tok.reshape(nb, bm), blk_e, pos.reshape(M, TOP_K).astype(jnp.int32)


def _pos_tiles(pos, tm):
    M = pos.shape[0]
    return pos.reshape(M // tm, tm, TOP_K).transpose(0, 2, 1).reshape(M // tm, tm * TOP_K)


def _stream_mixers(x, mods, pos, s_ret, h_rnn, conv_buf, lw, alpha):
    B, T, D = x.shape
    n_heads, d_ret = lw["n_heads"], lw["d_ret"]
    d_rnn = lw["d_rnn"]
    per_row = mods["per_row"]
    xg = x.reshape(1, B * T, D) if per_row else x
    h = ln_mod(xg, mods["sc1"], mods["sh1"]).reshape(B * T, D)
    proj = matmul(h, lw["w_in"]).reshape(B, T, -1)
    pre_ret, s_new = retention_mixer(proj, pos, s_ret, lw["ret_gn_w"], n_heads, d_ret)
    x_group = (4 * d_ret) // d_rnn
    pre_rnn, h_new, conv_new = rglru_mixer(proj, x_group, h_rnn[:, None, :], conv_buf, lw["conv_w"],
                                           lw["conv_b"], lw["w_a"], lw["b_a"], lw["w_x"], lw["b_x"],
                                           lw["rg_lambda"])
    mixin = merge_branches(pre_ret.reshape(B * T, d_ret), pre_rnn.reshape(B * T, d_rnn),
                           lw["w_br_ret"], lw["w_br_rnn"], proj.reshape(B * T, -1), 4 * d_ret + 2 * d_rnn)
    mix = matmul(mixin, lw["w_o"])
    x1, h2 = post_mixer(xg, mix.reshape(xg.shape), mods["g1"], mods["sc2"], mods["sh2"],
                        lw["ln1_w"], lw["ln1_b"], alpha)
    return x1, h2, s_new, h_new[:, 0, :], conv_new


def _split_mods(mod, B, T, per_row):
    D = mod.shape[1] // 6
    parts = mod.reshape(B, 6, D)
    out = {"per_row": per_row}
    for n, name in enumerate(("sh1", "sc1", "g1", "sh2", "sc2", "g2")):
        p = parts[:, n, :]
        out[name] = jnp.repeat(p, T, axis=0)[None] if per_row else p[:, None, :]
    return out


def kernel(x_prompt, x_sample, state_ret, state_rglru, state_conv, c_prompt, c_sample, w_ada, b_ada, w_in, ret_gn_w, conv_w, conv_b, w_a, b_a, w_x, b_x, rg_lambda, w_br_ret, w_br_rnn, w_o, ln1_w, ln1_b, w_router, router_bias, w1, w3, w2, ws1, ws3, ws2, ln2_w, ln2_b):
    B, T, D = x_prompt.shape
    Bs, Ts, _ = x_sample.shape
    depth = w_ada.shape[0]
    n_heads, dk, dv = state_ret.shape[2:]
    d_ret = n_heads * dk
    d_rnn = state_rglru.shape[-1]
    cw = conv_w.shape[1]
    n_experts = w_router.shape[-1]
    alpha = (2.0 * depth) ** 0.25
    pos_p = jnp.arange(T, dtype=jnp.int32)
    pos_s = PAST_LEN + jnp.arange(Ts, dtype=jnp.int32)
    row = lambda v: v.reshape(1, -1)

    yp, ys = x_prompt, x_sample
    outs = [[] for _ in range(6)]
    for l in range(depth):
        lw = dict(n_heads=n_heads, d_ret=d_ret, d_rnn=d_rnn,
                  w_in=w_in[l].astype(BF16), ret_gn_w=row(ret_gn_w[l]), conv_w=conv_w[l], conv_b=row(conv_b[l]),
                  w_a=w_a[l].astype(BF16), b_a=row(b_a[l]), w_x=w_x[l].astype(BF16), b_x=row(b_x[l]),
                  rg_lambda=row(rg_lambda[l]), w_br_ret=w_br_ret[l].astype(BF16),
                  w_br_rnn=w_br_rnn[l].astype(BF16), w_o=w_o[l].astype(BF16),
                  ln1_w=row(ln1_w[l]), ln1_b=row(ln1_b[l]))
        mod = ada_mod(jnp.concatenate([c_prompt, c_sample], axis=0), w_ada[l], row(b_ada[l]))
        mods_p = _split_mods(mod[:B], B, T, per_row=False)
        mods_s = _split_mods(mod[B:], Bs, Ts, per_row=True)

        zero_ret = jnp.zeros((B, n_heads, dk, dv), F32)
        zero_h = jnp.zeros((B, d_rnn), F32)
        zero_conv = jnp.zeros((B, cw - 1, d_rnn), F32)
        x1p, h2p, r1, h1, c1 = _stream_mixers(yp, mods_p, pos_p, zero_ret, zero_h, zero_conv, lw, alpha)
        x1s, h2s, r2, hh2, c2 = _stream_mixers(ys, mods_s, pos_s, state_ret[l], state_rglru[l],
                                               state_conv[l], lw, alpha)

        Mp, Ms = B * T, Bs * Ts
        h2_all = jnp.concatenate([h2p.reshape(Mp, D), h2s.reshape(Ms, D)], axis=0)
        idx_l, wt_l = router(h2_all, w_router[l].astype(BF16), row(router_bias[l]))
        slot_tok, blk_e, pos = _dispatch_tables(idx_l[:, :TOP_K], n_experts, EXPERT_ROWS)
        y_slots = routed_expert_blocks(h2_all, slot_tok, blk_e, w1[l].astype(BF16), w3[l].astype(BF16),
                                       w2[l].astype(BF16))
        tm_p = min(T, COMBINE_ROWS)
        tm_s = min(Ms, COMBINE_ROWS)
        ws = (ws1[l].astype(BF16), ws3[l].astype(BF16), ws2[l].astype(BF16))
        yp = combine_and_finish(x1p, h2p, wt_l[:Mp].reshape(B, T, LANES), _pos_tiles(pos[:Mp], tm_p), 0,
                                y_slots, mods_p["g2"], *ws, row(ln2_w[l]), row(ln2_b[l]), alpha)
        ys = combine_and_finish(x1s, h2s, wt_l[Mp:].reshape(1, Ms, LANES), _pos_tiles(pos[Mp:], tm_s), 0,
                                y_slots, mods_s["g2"], *ws, row(ln2_w[l]), row(ln2_b[l]), alpha)
        ys = ys.reshape(Bs, Ts, D)
        for lst, val in zip(outs, (r1, h1, c1, r2, hh2, c2)):
            lst.append(val)
    return (yp, ys) + tuple(jnp.stack(o) for o in outs)
```

```python
import functools
import math

import jax
import jax.numpy as jnp
from jax import lax
from jax.experimental import pallas as pl
from jax.experimental.pallas import tpu as pltpu

CHUNK = 64
TOP_K = 8
N_GROUPS = 8
TOPK_GROUPS = 4
ROUTED_SCALE = 2.5
RGLRU_C = 8.0
PAST_LEN = 4096
ROPE_BASE = 10000.0
LN_EPS = 1e-5

F32 = jnp.float32
BF16 = jnp.bfloat16

V7X_VMEM_LIMIT_BYTES = 56 * 1024 * 1024
LANES = 128
SUBLANES = 8
EXPERT_ROWS = 128
TOKEN_TILE = 128
CONV_PAD = 8


def _cparams(*sem):
    return pltpu.CompilerParams(dimension_semantics=sem, vmem_limit_bytes=V7X_VMEM_LIMIT_BYTES)


def _pick(n, cands):
    for c in cands:
        if n % c == 0:
            return c
    return n


def _ln(x):
    mu = jnp.mean(x, axis=-1, keepdims=True)
    xc = x - mu
    var = jnp.mean(xc * xc, axis=-1, keepdims=True)
    return xc * lax.rsqrt(var + LN_EPS)


def _sigmoid(x):
    return jax.nn.sigmoid(x)


def _dot(a, b):
    return jnp.dot(a, b, preferred_element_type=F32)


def _ada_kernel(c_ref, w_ref, b_ref, o_ref):
    c = c_ref[...]
    s = (c * _sigmoid(c)).astype(BF16)
    o_ref[...] = _dot(s, w_ref[...].astype(BF16)) + b_ref[...]


def ada_mod(c, w, b):
    R, D = c.shape
    N = w.shape[1]
    tn = _pick(N, (512, 256, 128))
    return pl.pallas_call(
        _ada_kernel,
        grid=(N // tn,),
        in_specs=[pl.BlockSpec((R, D), lambda j: (0, 0)),
                  pl.BlockSpec((D, tn), lambda j: (0, j)),
                  pl.BlockSpec((1, tn), lambda j: (0, j))],
        out_specs=pl.BlockSpec((R, tn), lambda j: (0, j)),
        out_shape=jax.ShapeDtypeStruct((R, N), F32),
        compiler_params=_cparams("arbitrary"),
        name="ada_mod",
    )(c, w, b)


def _mod_spec(arr, tm):
    D = arr.shape[-1]
    if arr.shape[1] == 1:
        return pl.BlockSpec((None, 1, D), lambda g, i: (g, 0, 0))
    return pl.BlockSpec((None, tm, D), lambda g, i: (g, i, 0))


def _lnmod_kernel(x_ref, sc_ref, sh_ref, o_ref):
    y = _ln(x_ref[...]) * (1.0 + sc_ref[...]) + sh_ref[...]
    o_ref[...] = y.astype(o_ref.dtype)


def ln_mod(x, sc, sh):
    G, R, D = x.shape
    tm = _pick(R, (512, 256, 128, 64, 32, 16, 8))
    row = pl.BlockSpec((None, tm, D), lambda g, i: (g, i, 0))
    return pl.pallas_call(
        _lnmod_kernel,
        grid=(G, R // tm),
        in_specs=[row, _mod_spec(sc, tm), _mod_spec(sh, tm)],
        out_specs=row,
        out_shape=jax.ShapeDtypeStruct((G, R, D), BF16),
        compiler_params=_cparams("parallel", "parallel"),
        name="ln_mod",
    )(x, sc, sh)


def _mm_kernel(a_ref, w_ref, o_ref):
    o_ref[...] = _dot(a_ref[...], w_ref[...]).astype(o_ref.dtype)


def matmul(a, w, out_dtype=F32):
    M, K = a.shape
    N = w.shape[1]
    tm = _pick(M, (1024, 512, 256, 128, 64, 32, 16, 8))
    tn = _pick(N, (1024, 512, 256, 128))
    return pl.pallas_call(
        _mm_kernel,
        grid=(M // tm, N // tn),
        in_specs=[pl.BlockSpec((tm, K), lambda i, j: (i, 0)),
                  pl.BlockSpec((K, tn), lambda i, j: (0, j))],
        out_specs=pl.BlockSpec((tm, tn), lambda i, j: (i, j)),
        out_shape=jax.ShapeDtypeStruct((M, N), out_dtype),
        compiler_params=_cparams("parallel", "parallel"),
        name="matmul",
    )(a, w)


def _rot(x, cos, sin):
    half = x.shape[-1] // 2
    x1 = x[:, :half]
    x2 = x[:, half:]
    return jnp.concatenate([x1 * cos - x2 * sin, x1 * sin + x2 * cos], axis=-1)


def _retention_kernel(q_ref, k_ref, v_ref, g_ref, cos_ref, sin_ref, intra_ref, kdec_ref, qdec_ref,
                      gch_ref, gnw_ref, s0_ref, o_ref, sfin_ref, s_scr, *, chunk, n_sub, q_scale):
    i = pl.program_id(2)

    @pl.when(i == 0)
    def _():
        s_scr[...] = s0_ref[...]

    intra = intra_ref[...]
    kdec = kdec_ref[...]
    qdec = qdec_ref[...]
    gch = gch_ref[...]
    gnw = gnw_ref[...]
    for c in range(n_sub):
        rows = pl.ds(c * chunk, chunk)
        cos = cos_ref[rows, :]
        sin = sin_ref[rows, :]
        q = _rot(q_ref[rows, :], cos, sin) * q_scale
        k = _rot(k_ref[rows, :], cos, sin)
        v = v_ref[rows, :].astype(BF16)
        s_prev = s_scr[...]
        scores = lax.dot_general(q.astype(BF16), k.astype(BF16), (((1,), (1,)), ((), ())),
                                 preferred_element_type=F32) * intra
        o = _dot(scores.astype(BF16), v) + _dot((q * qdec).astype(BF16), s_prev.astype(BF16))
        kv = lax.dot_general((k * kdec).astype(BF16), v, (((0,), (0,)), ((), ())),
                             preferred_element_type=F32)
        s_scr[...] = gch * s_prev + kv
        on = _ln(o) * gnw
        g = g_ref[rows, :]
        o_ref[rows, :] = (g * _sigmoid(g) * on).astype(o_ref.dtype)

    @pl.when(i == pl.num_programs(2) - 1)
    def _():
        sfin_ref[...] = s_scr[...]


def retention_mixer(proj, pos, s0, gn_w, n_heads, d_ret):
    B, T, _ = proj.shape
    dk = d_ret // n_heads
    dv = s0.shape[-1]
    chunk = min(T, CHUNK)
    n_chunks = T // chunk
    n_sub = _pick(n_chunks, (8, 4, 2, 1))
    tb = n_sub * chunk
    half = dk // 2
    inv = ROPE_BASE ** (-jnp.arange(half, dtype=F32) / half)
    ang = pos.astype(F32)[:, None] * inv[None, :]
    cos, sin = jnp.cos(ang), jnp.sin(ang)
    log_g = jnp.log1p(-jnp.exp2(-5.0 - jnp.arange(n_heads, dtype=F32)))
    ci = jnp.arange(chunk, dtype=F32)
    intra = jnp.exp(jnp.abs(ci[:, None] - ci[None, :])[None] * log_g[:, None, None])
    kdec = jnp.broadcast_to(jnp.exp((chunk - 1 - ci)[None, :] * log_g[:, None])[:, :, None],
                            (n_heads, chunk, dk))
    qdec = jnp.broadcast_to(jnp.exp((ci + 1.0)[None, :] * log_g[:, None])[:, :, None],
                            (n_heads, chunk, dk))
    gch = jnp.broadcast_to(jnp.exp(chunk * log_g)[:, None, None], (n_heads, 1, dv))

    def col(group):
        return pl.BlockSpec((None, tb, dk), lambda b, h, i: (b, i, group * n_heads + h))

    tab = pl.BlockSpec((tb, half), lambda b, h, i: (i, 0))
    per_head = lambda r, c: pl.BlockSpec((None, r, c), lambda b, h, i: (h, 0, 0))
    state = pl.BlockSpec((None, None, dk, dv), lambda b, h, i: (b, h, 0, 0))
    kern = functools.partial(_retention_kernel, chunk=chunk, n_sub=n_sub, q_scale=dk ** -0.5)
    return pl.pallas_call(
        kern,
        grid=(B, n_heads, T // tb),
        in_specs=[col(0), col(1), col(2), col(3), tab, tab,
                  per_head(chunk, chunk), per_head(chunk, dk), per_head(chunk, dk), per_head(1, dv),
                  pl.BlockSpec((1, dv), lambda b, h, i: (0, h)), state],
        out_specs=[pl.BlockSpec((None, tb, dv), lambda b, h, i: (b, i, h)), state],
        out_shape=[jax.ShapeDtypeStruct((B, T, n_heads * dv), BF16),
                   jax.ShapeDtypeStruct(s0.shape, F32)],
        scratch_shapes=[pltpu.VMEM((dk, dv), F32)],
        compiler_params=_cparams("parallel", "parallel", "arbitrary"),
        name="retention",
    )(proj, proj, proj, proj, cos, sin, intra, kdec, qdec, gch, gn_w, s0)


def _softplus(z):
    return jnp.maximum(z, 0.0) + jnp.log1p(jnp.exp(-jnp.abs(z)))


def _expm1(x):
    e = jnp.exp(x)
    em1 = e - 1.0
    return jnp.where(e == 1.0, x, jnp.where(em1 == -1.0, -1.0, em1 * x / jnp.log(e)))


def _gelu_tanh(x):
    return 0.5 * x * (1.0 + jnp.tanh(math.sqrt(2.0 / math.pi) * (x + 0.044715 * (x * x * x))))


def _rglru_kernel(x_ref, g_ref, cw_ref, cb_ref, wa_ref, ba_ref, wx_ref, bx_ref, lam_ref, h0_ref, c0_ref,
                  o_ref, hn_ref, cn_ref, xp_scr, a_scr, u_scr, h_scr, hc_scr, *, tb, conv_w, n_blocks):
    i = pl.program_id(1)
    hist = conv_w - 1
    lo = CONV_PAD - hist

    @pl.when(i == 0)
    def _():
        xp_scr[lo:CONV_PAD, :] = c0_ref[...]
        hc_scr[...] = h0_ref[...]

    @pl.when(i > 0)
    def _():
        xp_scr[lo:CONV_PAD, :] = xp_scr[tb + lo:tb + CONV_PAD, :]

    xp_scr[CONV_PAD:CONV_PAD + tb, :] = x_ref[...]
    xc = cb_ref[...] + xp_scr[lo:lo + tb, :] * cw_ref[0:1, :]
    for j in range(1, conv_w):
        xc = xc + xp_scr[lo + j:lo + j + tb, :] * cw_ref[j:j + 1, :]

    sp = RGLRU_C * _softplus(-lam_ref[...])
    bw = xc.shape[1] // n_blocks
    for n in range(n_blocks):
        cols = slice(n * bw, (n + 1) * bw)
        xb = xc[:, cols]
        xb16 = xb.astype(BF16)
        r = _sigmoid(_dot(xb16, wa_ref[n]) + ba_ref[:, cols])
        ig = _sigmoid(_dot(xb16, wx_ref[n]) + bx_ref[:, cols])
        log_a = -(r * sp[:, cols])
        a_scr[:, cols] = jnp.exp(log_a)
        u_scr[:, cols] = jnp.sqrt(-_expm1(2.0 * log_a)) * (ig * xb)

    def step(t, h):
        h = a_scr[pl.ds(t, 1), :] * h + u_scr[pl.ds(t, 1), :]
        h_scr[pl.ds(t, 1), :] = h
        return h

    h_last = lax.fori_loop(0, tb, step, hc_scr[...], unroll=8)
    hc_scr[...] = h_last
    o_ref[...] = (_gelu_tanh(g_ref[...]) * h_scr[...]).astype(o_ref.dtype)

    @pl.when(i == pl.num_programs(1) - 1)
    def _():
        hn_ref[...] = h_last
        cn_ref[...] = xp_scr[tb + lo:tb + CONV_PAD, :]


def rglru_mixer(proj, x_group, h0, conv0, conv_w, conv_b, w_a, b_a, w_x, b_x, lam):
    B, T, _ = proj.shape
    d_rnn = h0.shape[-1]
    cw = conv_w.shape[0]
    n_blocks = w_a.shape[0]
    tb = _pick(T, (256, 128, 64, 32, 16, 8))
    kern = functools.partial(_rglru_kernel, tb=tb, conv_w=cw, n_blocks=n_blocks)
    full = lambda *s: pl.BlockSpec(s, lambda b, i: (0,) * len(s))
    return pl.pallas_call(
        kern,
        grid=(B, T // tb),
        in_specs=[pl.BlockSpec((None, tb, d_rnn), lambda b, i: (b, i, x_group)),
                  pl.BlockSpec((None, tb, d_rnn), lambda b, i: (b, i, x_group + 1)),
                  full(cw, d_rnn), full(1, d_rnn),
                  full(*w_a.shape), full(1, d_rnn), full(*w_x.shape), full(1, d_rnn), full(1, d_rnn),
                  pl.BlockSpec((None, 1, d_rnn), lambda b, i: (b, 0, 0)),
                  pl.BlockSpec((None, cw - 1, d_rnn), lambda b, i: (b, 0, 0))],
        out_specs=[pl.BlockSpec((None, tb, d_rnn), lambda b, i: (b, i, 0)),
                   pl.BlockSpec((None, 1, d_rnn), lambda b, i: (b, 0, 0)),
                   pl.BlockSpec((None, cw - 1, d_rnn), lambda b, i: (b, 0, 0))],
        out_shape=[jax.ShapeDtypeStruct((B, T, d_rnn), BF16),
                   jax.ShapeDtypeStruct((B, 1, d_rnn), F32),
                   jax.ShapeDtypeStruct((B, cw - 1, d_rnn), F32)],
        scratch_shapes=[pltpu.VMEM((tb + CONV_PAD, d_rnn), F32), pltpu.VMEM((tb, d_rnn), F32),
                        pltpu.VMEM((tb, d_rnn), F32), pltpu.VMEM((tb, d_rnn), F32),
                        pltpu.VMEM((1, d_rnn), F32)],
        compiler_params=_cparams("parallel", "arbitrary"),
        name="rglru",
    )(proj, proj, conv_w, conv_b, w_a, b_a, w_x, b_x, lam, h0, conv0)


def _merge_kernel(pr_ref, pn_ref, wr_ref, wn_ref, g0_ref, g1_ref, o_ref):
    yr = _dot(pr_ref[...], wr_ref[...])
    yn = _dot(pn_ref[...], wn_ref[...])
    o_ref[...] = (_sigmoid(g0_ref[...]) * yr + _sigmoid(g1_ref[...]) * yn).astype(o_ref.dtype)


def merge_branches(pre_ret, pre_rnn, w_ret, w_rnn, proj, gate_off):
    M, kr = pre_ret.shape
    kn = pre_rnn.shape[1]
    D = w_ret.shape[1]
    tm = _pick(M, (1024, 512, 256, 128, 64, 32, 16, 8))
    tn = _pick(math.gcd(D, gate_off), (512, 256, 128))
    g0 = gate_off // tn
    g1 = (gate_off + D) // tn
    return pl.pallas_call(
        _merge_kernel,
        grid=(M // tm, D // tn),
        in_specs=[pl.BlockSpec((tm, kr), lambda i, j: (i, 0)),
                  pl.BlockSpec((tm, kn), lambda i, j: (i, 0)),
                  pl.BlockSpec((kr, tn), lambda i, j: (0, j)),
                  pl.BlockSpec((kn, tn), lambda i, j: (0, j)),
                  pl.BlockSpec((tm, tn), lambda i, j: (i, g0 + j)),
                  pl.BlockSpec((tm, tn), lambda i, j: (i, g1 + j))],
        out_specs=pl.BlockSpec((tm, tn), lambda i, j: (i, j)),
        out_shape=jax.ShapeDtypeStruct((M, D), BF16),
        compiler_params=_cparams("parallel", "parallel"),
        name="merge_branches",
    )(pre_ret, pre_rnn, w_ret, w_rnn, proj, proj)


def _pack_pair(x):
    half = x.shape[-1] // 2
    lo = lax.bitcast_convert_type(x[:, :half].astype(BF16).astype(F32), jnp.uint32) >> 16
    hi = lax.bitcast_convert_type(x[:, half:].astype(BF16).astype(F32), jnp.uint32) & jnp.uint32(0xFFFF0000)
    return lo | hi


def _unpack_pair(w):
    lo = lax.bitcast_convert_type(w << 16, F32)
    hi = lax.bitcast_convert_type(w & jnp.uint32(0xFFFF0000), F32)
    return lo, hi


def _post1_kernel(x_ref, mix_ref, g1_ref, sc_ref, sh_ref, w_ref, b_ref, x1_ref, h2_ref, *, alpha):
    x1 = _ln(alpha * x_ref[...] + g1_ref[...] * mix_ref[...]) * w_ref[...] + b_ref[...]
    x1_ref[...] = x1
    h2_ref[...] = _pack_pair(_ln(x1) * (1.0 + sc_ref[...]) + sh_ref[...])


def post_mixer(x, mix, g1, sc2, sh2, ln_w, ln_b, alpha):
    G, R, D = x.shape
    tm = _pick(R, (256, 128, 64, 32, 16, 8))
    row = pl.BlockSpec((None, tm, D), lambda g, i: (g, i, 0))
    vec = pl.BlockSpec((1, D), lambda g, i: (0, 0))
    return pl.pallas_call(
        functools.partial(_post1_kernel, alpha=alpha),
        grid=(G, R // tm),
        in_specs=[row, row, _mod_spec(g1, tm), _mod_spec(sc2, tm), _mod_spec(sh2, tm), vec, vec],
        out_specs=[row, pl.BlockSpec((None, tm, D // 2), lambda g, i: (g, i, 0))],
        out_shape=[jax.ShapeDtypeStruct((G, R, D), F32), jax.ShapeDtypeStruct((G, R, D // 2), jnp.uint32)],
        compiler_params=_cparams("parallel", "parallel"),
        name="post_mixer",
    )(x, mix, g1, sc2, sh2, ln_w, ln_b)


def _router_kernel(h_ref, wlo_ref, whi_ref, b_ref, idx_ref, wt_ref, rank_ref, cnt_ref, carry_scr, *, n_experts):
    tm = h_ref.shape[0]
    gsz = n_experts // N_GROUPS
    neg = -jnp.inf

    @pl.when(pl.program_id(0) == 0)
    def _():
        carry_scr[...] = jnp.zeros_like(carry_scr)

    h_lo, h_hi = _unpack_pair(h_ref[...])
    s = _sigmoid(_dot(h_lo.astype(BF16), wlo_ref[...]) + _dot(h_hi.astype(BF16), whi_ref[...]))
    sb = s + b_ref[...]
    lane = lax.broadcasted_iota(jnp.int32, (tm, n_experts), 1).astype(F32)
    grp = jnp.zeros((tm, n_experts), F32)
    for g in range(1, N_GROUPS):
        grp = grp + (lane >= float(g * gsz)).astype(F32)
    lane_o = lax.broadcasted_iota(jnp.int32, (tm, LANES), 1).astype(F32)

    def argmax_first(v, ids, big):
        m = jnp.max(v, axis=-1, keepdims=True)
        first = jnp.min(jnp.where(v == m, ids, float(big)), axis=-1, keepdims=True)
        return m, first

    gscore = jnp.full((tm, LANES), neg, F32)
    for g in range(N_GROUPS):
        vg = jnp.where(grp == g, sb, neg)
        m1, i1 = argmax_first(vg, lane, n_experts)
        m2 = jnp.max(jnp.where(lane == i1, neg, vg), axis=-1, keepdims=True)
        gscore = jnp.where(lane_o == g, m1 + m2, gscore)
    keep = jnp.zeros((tm, n_experts), jnp.bool_)
    for _ in range(TOPK_GROUPS):
        _, gi = argmax_first(gscore, lane_o, LANES)
        gscore = jnp.where(lane_o == gi, neg, gscore)
        keep = jnp.logical_or(keep, grp == gi)
    sel = jnp.where(keep, sb, neg)
    idx_o = jnp.zeros((tm, LANES), F32)
    wt_o = jnp.zeros((tm, LANES), F32)
    chosen = []
    taken = jnp.zeros((tm, n_experts), F32)
    for k in range(TOP_K):
        _, ei = argmax_first(sel, lane, n_experts)
        hit = lane == ei
        sel = jnp.where(hit, neg, sel)
        wk = jnp.sum(jnp.where(hit, s, 0.0), axis=-1, keepdims=True)
        idx_o = jnp.where(lane_o == k, ei, idx_o)
        wt_o = jnp.where(lane_o == k, wk, wt_o)
        taken = jnp.where(hit, 1.0, taken)
        chosen.append(ei)
    wt_o = wt_o / jnp.sum(wt_o, axis=-1, keepdims=True) * ROUTED_SCALE
    idx_ref[...] = idx_o.astype(jnp.int32)
    wt_ref[...] = wt_o
    ri = lax.broadcasted_iota(jnp.int32, (tm, tm), 0)
    ci = lax.broadcasted_iota(jnp.int32, (tm, tm), 1)
    before = _dot((ri > ci).astype(BF16), taken.astype(BF16)) + carry_scr[...]
    rank_o = jnp.zeros((tm, LANES), F32)
    for k in range(TOP_K):
        rk = jnp.sum(jnp.where(lane == chosen[k], before, 0.0), axis=-1, keepdims=True)
        rank_o = jnp.where(lane_o == k, rk, rank_o)
    rank_ref[...] = rank_o.astype(jnp.int32)
    carry_scr[...] = carry_scr[...] + jnp.sum(taken, axis=0, keepdims=True)
    cnt_ref[...] = carry_scr[...]


def router(h2p, w_router, bias):
    M, H = h2p.shape
    E = w_router.shape[1]
    tm = _pick(M, (256, 128, 64, 32, 16, 8))
    out = pl.BlockSpec((tm, LANES), lambda i: (i, 0))
    return pl.pallas_call(
        functools.partial(_router_kernel, n_experts=E),
        grid=(M // tm,),
        in_specs=[pl.BlockSpec((tm, H), lambda i: (i, 0)),
                  pl.BlockSpec((H, E), lambda i: (0, 0)),
                  pl.BlockSpec((H, E), lambda i: (1, 0)),
                  pl.BlockSpec((1, E), lambda i: (0, 0))],
        out_specs=[out, out, out, pl.BlockSpec((1, E), lambda i: (0, 0))],
        out_shape=[jax.ShapeDtypeStruct((M, LANES), jnp.int32), jax.ShapeDtypeStruct((M, LANES), F32),
                   jax.ShapeDtypeStruct((M, LANES), jnp.int32), jax.ShapeDtypeStruct((1, E), F32)],
        scratch_shapes=[pltpu.VMEM((1, E), F32)],
        compiler_params=_cparams("arbitrary"),
        name="router",
    )(h2p, w_router, w_router, bias)


def _row_copy(src, src_row, dst, dst_row, sem):
    return pltpu.make_async_copy(src.at[pl.ds(src_row, 1)], dst.at[pl.ds(dst_row, 1)], sem)


def _tile_copies(idx_hbm, rank_hbm, idx_smem, rank_smem, isem, tile, s):
    return (pltpu.make_async_copy(idx_hbm.at[tile], idx_smem.at[s], isem.at[0, s]),
            pltpu.make_async_copy(rank_hbm.at[tile], rank_smem.at[s], isem.at[1, s]))


def _pad_chunks(rows):
    n, out = rows // 2, []
    while n >= SUBLANES:
        out.append(n)
        n //= 2
    return out


def _dispatch_kernel(pstart_ref, count_ref, padded_ref, nused_ref, idx_hbm, rank_hbm, h_hbm, xs_hbm,
                     idx_smem, rank_smem, zbuf, isem, dsem, zsem, *, tm, n_experts, bm):
    i = pl.program_id(0)
    nt = pl.num_programs(0)
    slot = i % 2
    nxt = 1 - slot
    n_rows = tm * TOP_K

    def pad_fill(wait):
        def body(e, c):
            npad = padded_ref[e] - count_ref[e]
            off = pstart_ref[e] + count_ref[e]
            n_single = (-count_ref[e]) & (SUBLANES - 1)
            for j in range(SUBLANES - 1):
                cp = _row_copy(zbuf, 0, xs_hbm, off + j, zsem)

                @pl.when(j < n_single)
                def _():
                    cp.wait() if wait else cp.start()

            off = pl.multiple_of(off + n_single, SUBLANES)
            n_tiled = npad - n_single
            for n in _pad_chunks(bm):
                cp = pltpu.make_async_copy(zbuf.at[pl.ds(0, n)], xs_hbm.at[pl.ds(off, n)], zsem)

                @pl.when((n_tiled & n) != 0)
                def _():
                    cp.wait() if wait else cp.start()

                off = pl.multiple_of(off + (n_tiled & n), SUBLANES)
            return c
        lax.fori_loop(0, n_experts, body, 0)

        def tail(b, c):
            for part in range(2):
                row0 = pl.multiple_of(b * bm + part * (bm // 2), SUBLANES)
                cp = pltpu.make_async_copy(zbuf, xs_hbm.at[pl.ds(row0, bm // 2)], zsem)
                cp.wait() if wait else cp.start()
            return c
        lax.fori_loop(nused_ref[0], xs_hbm.shape[0] // bm, tail, 0)

    @pl.when(i == 0)
    def _():
        zbuf[...] = jnp.zeros_like(zbuf)
        pad_fill(False)
        for cp in _tile_copies(idx_hbm, rank_hbm, idx_smem, rank_smem, isem, 0, 0):
            cp.start()
            cp.wait()
        pad_fill(True)

    @pl.when(i + 1 < nt)
    def _():
        for cp in _tile_copies(idx_hbm, rank_hbm, idx_smem, rank_smem, isem, i + 1, nxt):
            cp.start()

    def issue(r, c):
        for k in range(TOP_K):
            a = r * TOP_K + k
            p = pstart_ref[idx_smem[slot, a]] + rank_smem[slot, a]
            _row_copy(h_hbm, i * tm + r, xs_hbm, p, dsem.at[slot]).start()
        return c
    lax.fori_loop(0, tm, issue, 0)

    def wait_all(s):
        def body(r, c):
            _row_copy(h_hbm, 0, xs_hbm, 0, dsem.at[s]).wait()
            return c
        lax.fori_loop(0, n_rows, body, 0, unroll=8)

    @pl.when(i > 0)
    def _():
        wait_all(nxt)

    @pl.when(i + 1 < nt)
    def _():
        for cp in _tile_copies(idx_hbm, rank_hbm, idx_smem, rank_smem, isem, i + 1, nxt):
            cp.wait()

    @pl.when(i == nt - 1)
    def _():
        wait_all(slot)


def dispatch_tokens(h2p, idx_tiles, rank_tiles, pstart, counts, padded, n_used, n_slots, bm):
    M, H = h2p.shape
    nt, n_rows = idx_tiles.shape
    tm = n_rows // TOP_K
    any_spec = pl.BlockSpec(memory_space=pl.ANY)
    grid_spec = pltpu.PrefetchScalarGridSpec(
        num_scalar_prefetch=4,
        grid=(nt,),
        in_specs=[any_spec, any_spec, any_spec],
        out_specs=any_spec,
        scratch_shapes=[pltpu.SMEM((2, n_rows), jnp.int32), pltpu.SMEM((2, n_rows), jnp.int32),
                        pltpu.VMEM((bm // 2, H), jnp.uint32),
                        pltpu.SemaphoreType.DMA((2, 2)), pltpu.SemaphoreType.DMA((2,)),
                        pltpu.SemaphoreType.DMA],
    )
    return pl.pallas_call(
        functools.partial(_dispatch_kernel, tm=tm, n_experts=pstart.shape[0], bm=bm),
        grid_spec=grid_spec,
        out_shape=jax.ShapeDtypeStruct((n_slots, H), jnp.uint32),
        compiler_params=_cparams("arbitrary"),
        name="dispatch",
    )(pstart, counts, padded, n_used, idx_tiles, rank_tiles, h2p)


def _experts_kernel(blk_e_ref, first_ref, next_ref, nused_ref, xs_ref, w1_hbm, w3_hbm, w2_hbm, y_ref,
                    st1, st3, st2, wb1, wb3, wb2, wsem):
    i = pl.program_id(0)
    half = xs_ref.shape[1]

    def weight_copies(e):
        return (pltpu.make_async_copy(w1_hbm.at[e], st1, wsem.at[0]),
                pltpu.make_async_copy(w3_hbm.at[e], st3, wsem.at[1]),
                pltpu.make_async_copy(w2_hbm.at[e], st2, wsem.at[2]))

    @pl.when(i == 0)
    def _():
        for cp in weight_copies(blk_e_ref[0]):
            cp.start()

    @pl.when(first_ref[i] == 1)
    def _():
        for cp in weight_copies(blk_e_ref[i]):
            cp.wait()
        wb1[...] = st1[...].astype(BF16)
        wb3[...] = st3[...].astype(BF16)
        wb2[...] = st2[...].astype(BF16)

        @pl.when(next_ref[i] >= 0)
        def _():
            for cp in weight_copies(next_ref[i]):
                cp.start()

    @pl.when(i < nused_ref[0])
    def _():
        x_lo, x_hi = _unpack_pair(xs_ref[...])
        x_lo = x_lo.astype(BF16)
        x_hi = x_hi.astype(BF16)
        h1 = _dot(x_lo, wb1[0:half, :]) + _dot(x_hi, wb1[half:2 * half, :])
        h3 = _dot(x_lo, wb3[0:half, :]) + _dot(x_hi, wb3[half:2 * half, :])
        hh = (h1 * _sigmoid(h1) * h3).astype(BF16)
        y_ref[...] = _pack_pair(_dot(hh, wb2[...]))

    @pl.when(i >= nused_ref[0])
    def _():
        y_ref[...] = jnp.zeros_like(y_ref)


def routed_expert_blocks(xs, blk_e, first, next_e, n_used, w1, w3, w2, bm):
    n_slots, H = xs.shape
    nb = n_slots // bm
    _, D, F = w1.shape
    any_spec = pl.BlockSpec(memory_space=pl.ANY)
    grid_spec = pltpu.PrefetchScalarGridSpec(
        num_scalar_prefetch=4,
        grid=(nb,),
        in_specs=[pl.BlockSpec((bm, H), lambda i, be, fi, ne, nu: (jnp.minimum(i, nu[0] - 1), 0)),
                  any_spec, any_spec, any_spec],
        out_specs=pl.BlockSpec((bm, H), lambda i, be, fi, ne, nu: (i, 0)),
        scratch_shapes=[pltpu.VMEM((D, F), F32), pltpu.VMEM((D, F), F32), pltpu.VMEM((F, D), F32),
                        pltpu.VMEM((D, F), BF16), pltpu.VMEM((D, F), BF16), pltpu.VMEM((F, D), BF16),
                        pltpu.SemaphoreType.DMA((3,))],
    )
    return pl.pallas_call(
        _experts_kernel,
        grid_spec=grid_spec,
        out_shape=jax.ShapeDtypeStruct((n_slots, H), jnp.uint32),
        compiler_params=_cparams("arbitrary"),
        name="routed_experts",
    )(blk_e, first, next_e, n_used, xs, w1, w3, w2)


def _final_kernel(pstart_ref, idx_hbm, rank_hbm, y_hbm, x1_ref, h2_ref, wt_ref, g2_ref, ws1_ref, ws3_ref, ws2_ref,
                  w_ref, b_ref, o_ref, ybuf, idx_smem, rank_smem, gsem, isem, *, tm, alpha, tile_off):
    g = pl.program_id(0)
    i = pl.program_id(1)
    nt = pl.num_programs(1)
    step = g * nt + i
    nsteps = pl.num_programs(0) * nt
    slot = step % 2
    nxt = 1 - slot
    n_rows = tm * TOP_K
    half = h2_ref.shape[1]

    def tile_copies(st, s):
        return _tile_copies(idx_hbm, rank_hbm, idx_smem, rank_smem, isem, tile_off + st, s)

    def issue_rows(s):
        def body(r, c):
            for k in range(TOP_K):
                a = r * TOP_K + k
                p = pstart_ref[idx_smem[s, a]] + rank_smem[s, a]
                _row_copy(y_hbm, p, ybuf.at[s], k * tm + r, gsem.at[s]).start()
            return c
        lax.fori_loop(0, tm, body, 0)

    def wait_rows(s):
        def body(r, c):
            _row_copy(y_hbm, 0, ybuf.at[s], r, gsem.at[s]).wait()
            return c
        lax.fori_loop(0, n_rows, body, 0, unroll=8)

    @pl.when(step == 0)
    def _():
        for cp in tile_copies(0, 0):
            cp.start()
            cp.wait()
        issue_rows(0)

        @pl.when(nsteps > 1)
        def _():
            for cp in tile_copies(1, 1):
                cp.start()

    @pl.when(step + 1 < nsteps)
    def _():
        for cp in tile_copies(step + 1, nxt):
            cp.wait()
        issue_rows(nxt)

    h_lo, h_hi = _unpack_pair(h2_ref[...])
    h_lo = h_lo.astype(BF16)
    h_hi = h_hi.astype(BF16)
    a1 = _dot(h_lo, ws1_ref[0:half, :]) + _dot(h_hi, ws1_ref[half:2 * half, :])
    a3 = _dot(h_lo, ws3_ref[0:half, :]) + _dot(h_hi, ws3_ref[half:2 * half, :])
    shared = _dot((a1 * _sigmoid(a1) * a3).astype(BF16), ws2_ref[...])
    ffn_lo = shared[:, :half]
    ffn_hi = shared[:, half:]

    wait_rows(slot)

    @pl.when(step + 2 < nsteps)
    def _():
        for cp in tile_copies(step + 2, slot):
            cp.start()

    wt = wt_ref[...]
    for k in range(TOP_K):
        y_lo, y_hi = _unpack_pair(ybuf[slot, k * tm:(k + 1) * tm, :])
        wk = wt[:, k:k + 1]
        ffn_lo = ffn_lo + wk * y_lo
        ffn_hi = ffn_hi + wk * y_hi
    ffn = jnp.concatenate([ffn_lo, ffn_hi], axis=-1)
    o_ref[...] = _ln(alpha * x1_ref[...] + g2_ref[...] * ffn) * w_ref[...] + b_ref[...]


def combine_and_finish(x1, h2p, wts, idx_tiles, rank_tiles, tile_off, pstart, y, g2, ws1, ws3, ws2,
                       ln_w, ln_b, alpha):
    G, R, D = x1.shape
    H = D // 2
    tm = idx_tiles.shape[1] // TOP_K
    F = ws1.shape[1]
    any_spec = pl.BlockSpec(memory_space=pl.ANY)
    row = lambda w: pl.BlockSpec((None, tm, w), lambda g, i, ps: (g, i, 0))
    const = lambda r, c: pl.BlockSpec((r, c), lambda g, i, ps: (0, 0))
    if g2.shape[1] == 1:
        g2_spec = pl.BlockSpec((None, 1, D), lambda g, i, ps: (g, 0, 0))
    else:
        g2_spec = row(D)
    kern = functools.partial(_final_kernel, tm=tm, alpha=alpha, tile_off=tile_off)
    grid_spec = pltpu.PrefetchScalarGridSpec(
        num_scalar_prefetch=1,
        grid=(G, R // tm),
        in_specs=[any_spec, any_spec, any_spec, row(D), row(H), row(LANES), g2_spec,
                  const(D, F), const(D, F), const(F, D), const(1, D), const(1, D)],
        out_specs=row(D),
        scratch_shapes=[pltpu.VMEM((2, tm * TOP_K, H), jnp.uint32),
                        pltpu.SMEM((2, tm * TOP_K), jnp.int32), pltpu.SMEM((2, tm * TOP_K), jnp.int32),
                        pltpu.SemaphoreType.DMA((2,)), pltpu.SemaphoreType.DMA((2, 2))],
    )
    return pl.pallas_call(
        kern,
        grid_spec=grid_spec,
        out_shape=jax.ShapeDtypeStruct((G, R, D), F32),
        compiler_params=_cparams("arbitrary", "arbitrary"),
        name="combine_finish",
    )(pstart, idx_tiles, rank_tiles, y, x1, h2p, wts, g2, ws1, ws3, ws2, ln_w, ln_b)


def _block_tables(counts, n_blocks, bm):
    E = counts.shape[0]
    padded = (counts + bm - 1) // bm * bm
    pend = jnp.cumsum(padded)
    pstart = pend - padded
    n_used = pend[-1] // bm
    blk = jnp.arange(n_blocks, dtype=jnp.int32)
    blk_e = jnp.minimum(jnp.searchsorted(pend, blk * bm, side="right"), E - 1).astype(jnp.int32)
    prev_e = jnp.concatenate([jnp.full((1,), -1, jnp.int32), blk_e[:-1]])
    first = ((blk_e != prev_e) & (blk < n_used)).astype(jnp.int32)
    ids = jnp.arange(E, dtype=jnp.int32)
    active = jnp.where(counts > 0, ids, E)
    later = lax.cummin(active[::-1])[::-1]
    nxt = jnp.concatenate([later[1:], jnp.full((1,), E, jnp.int32)])
    nxt = jnp.where(nxt >= E, -1, nxt)
    next_e = nxt[blk_e].astype(jnp.int32)
    i32 = lambda v: v.astype(jnp.int32)
    return i32(pstart), i32(padded), blk_e, first, next_e, i32(n_used).reshape(1)


def _assignment_tiles(v, tm):
    M = v.shape[0]
    return v[:, :TOP_K].reshape(M // tm, tm * TOP_K)


def _stream_mixers(x, mods, pos, s_ret, h_rnn, conv_buf, lw, alpha):
    B, T, D = x.shape
    n_heads, d_ret = lw["n_heads"], lw["d_ret"]
    d_rnn = lw["d_rnn"]
    per_row = mods["per_row"]
    xg = x.reshape(1, B * T, D) if per_row else x
    h = ln_mod(xg, mods["sc1"], mods["sh1"]).reshape(B * T, D)
    proj = matmul(h, lw["w_in"]).reshape(B, T, -1)
    pre_ret, s_new = retention_mixer(proj, pos, s_ret, lw["ret_gn_w"], n_heads, d_ret)
    x_group = (4 * d_ret) // d_rnn
    pre_rnn, h_new, conv_new = rglru_mixer(proj, x_group, h_rnn[:, None, :], conv_buf, lw["conv_w"],
                                           lw["conv_b"], lw["w_a"], lw["b_a"], lw["w_x"], lw["b_x"],
                                           lw["rg_lambda"])
    mixin = merge_branches(pre_ret.reshape(B * T, d_ret), pre_rnn.reshape(B * T, d_rnn),
                           lw["w_br_ret"], lw["w_br_rnn"], proj.reshape(B * T, -1), 4 * d_ret + 2 * d_rnn)
    mix = matmul(mixin, lw["w_o"])
    x1, h2 = post_mixer(xg, mix.reshape(xg.shape), mods["g1"], mods["sc2"], mods["sh2"],
                        lw["ln1_w"], lw["ln1_b"], alpha)
    return x1, h2, s_new, h_new[:, 0, :], conv_new


def _split_mods(mod, B, T, per_row):
    D = mod.shape[1] // 6
    parts = mod.reshape(B, 6, D)
    out = {"per_row": per_row}
    for n, name in enumerate(("sh1", "sc1", "g1", "sh2", "sc2", "g2")):
        p = parts[:, n, :]
        out[name] = jnp.repeat(p, T, axis=0)[None] if per_row else p[:, None, :]
    return out


def kernel(x_prompt, x_sample, state_ret, state_rglru, state_conv, c_prompt, c_sample, w_ada, b_ada, w_in, ret_gn_w, conv_w, conv_b, w_a, b_a, w_x, b_x, rg_lambda, w_br_ret, w_br_rnn, w_o, ln1_w, ln1_b, w_router, router_bias, w1, w3, w2, ws1, ws3, ws2, ln2_w, ln2_b):
    B, T, D = x_prompt.shape
    Bs, Ts, _ = x_sample.shape
    depth = w_ada.shape[0]
    n_heads, dk, dv = state_ret.shape[2:]
    d_ret = n_heads * dk
    d_rnn = state_rglru.shape[-1]
    cw = conv_w.shape[1]
    n_experts = w_router.shape[-1]
    alpha = (2.0 * depth) ** 0.25
    pos_p = jnp.arange(T, dtype=jnp.int32)
    pos_s = PAST_LEN + jnp.arange(Ts, dtype=jnp.int32)
    row = lambda v: v.reshape(1, -1)

    yp, ys = x_prompt, x_sample
    outs = [[] for _ in range(6)]
    for l in range(depth):
        lw = dict(n_heads=n_heads, d_ret=d_ret, d_rnn=d_rnn,
                  w_in=w_in[l].astype(BF16), ret_gn_w=row(ret_gn_w[l]), conv_w=conv_w[l], conv_b=row(conv_b[l]),
                  w_a=w_a[l].astype(BF16), b_a=row(b_a[l]), w_x=w_x[l].astype(BF16), b_x=row(b_x[l]),
                  rg_lambda=row(rg_lambda[l]), w_br_ret=w_br_ret[l].astype(BF16),
                  w_br_rnn=w_br_rnn[l].astype(BF16), w_o=w_o[l].astype(BF16),
                  ln1_w=row(ln1_w[l]), ln1_b=row(ln1_b[l]))
        mod = ada_mod(jnp.concatenate([c_prompt, c_sample], axis=0), w_ada[l], row(b_ada[l]))
        mods_p = _split_mods(mod[:B], B, T, per_row=False)
        mods_s = _split_mods(mod[B:], Bs, Ts, per_row=True)

        zero_ret = jnp.zeros((B, n_heads, dk, dv), F32)
        zero_h = jnp.zeros((B, d_rnn), F32)
        zero_conv = jnp.zeros((B, cw - 1, d_rnn), F32)
        x1p, h2p, r1, h1, c1 = _stream_mixers(yp, mods_p, pos_p, zero_ret, zero_h, zero_conv, lw, alpha)
        x1s, h2s, r2, hh2, c2 = _stream_mixers(ys, mods_s, pos_s, state_ret[l], state_rglru[l],
                                               state_conv[l], lw, alpha)

        Mp, Ms = B * T, Bs * Ts
        M = Mp + Ms
        bm = EXPERT_ROWS
        tm = math.gcd(math.gcd(T, Ms), TOKEN_TILE)
        h2_all = jnp.concatenate([h2p.reshape(Mp, D // 2), h2s.reshape(Ms, D // 2)], axis=0)
        idx_l, wt_l, rank_l, cnt = router(h2_all, w_router[l].astype(BF16), row(router_bias[l]))
        n_blocks = -(-(M * TOP_K) // bm) + n_experts
        pstart, padded, blk_e, first, next_e, n_used = _block_tables(cnt[0].astype(jnp.int32), n_blocks, bm)
        idx_t = _assignment_tiles(idx_l, tm)
        rank_t = _assignment_tiles(rank_l, tm)
        xs = dispatch_tokens(h2_all, idx_t, rank_t, pstart, cnt[0].astype(jnp.int32), padded, n_used,
                             n_blocks * bm, bm)
        y_slots = routed_expert_blocks(xs, blk_e, first, next_e, n_used, w1[l], w3[l], w2[l], bm)
        ws = (ws1[l].astype(BF16), ws3[l].astype(BF16), ws2[l].astype(BF16))
        yp = combine_and_finish(x1p, h2p, wt_l[:Mp].reshape(B, T, LANES), idx_t, rank_t, 0, pstart,
                                y_slots, mods_p["g2"], *ws, row(ln2_w[l]), row(ln2_b[l]), alpha)
        ys = combine_and_finish(x1s, h2s, wt_l[Mp:].reshape(1, Ms, LANES), idx_t, rank_t, Mp // tm, pstart,
                                y_slots, mods_s["g2"], *ws, row(ln2_w[l]), row(ln2_b[l]), alpha)
        ys = ys.reshape(Bs, Ts, D)
        for lst, val in zip(outs, (r1, h1, c1, r2, hh2, c2)):
            lst.append(val)
    return (yp, ys) + tuple(jnp.stack(o) for o in outs)
```

```python
import functools
import math

import jax
import jax.numpy as jnp
from jax import lax
from jax.experimental import pallas as pl
from jax.experimental.pallas import tpu as pltpu

CHUNK = 64
TOP_K = 8
N_GROUPS = 8
TOPK_GROUPS = 4
ROUTED_SCALE = 2.5
RGLRU_C = 8.0
PAST_LEN = 4096
ROPE_BASE = 10000.0
LN_EPS = 1e-5

F32 = jnp.float32
BF16 = jnp.bfloat16

V7X_VMEM_LIMIT_BYTES = 56 * 1024 * 1024
LANES = 128
SUBLANES = 8
EXPERT_ROWS = 256
TOKEN_TILE = 128
CONV_PAD = 8


def _cparams(*sem):
    return pltpu.CompilerParams(dimension_semantics=sem, vmem_limit_bytes=V7X_VMEM_LIMIT_BYTES)


def _pick(n, cands):
    for c in cands:
        if n % c == 0:
            return c
    return n


def _ln(x):
    mu = jnp.mean(x, axis=-1, keepdims=True)
    xc = x - mu
    var = jnp.mean(xc * xc, axis=-1, keepdims=True)
    return xc * lax.rsqrt(var + LN_EPS)


def _sigmoid(x):
    return jax.nn.sigmoid(x)


def _dot(a, b):
    return jnp.dot(a, b, preferred_element_type=F32)


def _ada_kernel(c_ref, w_ref, b_ref, o_ref):
    c = c_ref[...]
    s = (c * _sigmoid(c)).astype(BF16)
    o_ref[...] = _dot(s, w_ref[...].astype(BF16)) + b_ref[...]


def ada_mod(c, w, b):
    R, D = c.shape
    N = w.shape[1]
    tn = _pick(N, (512, 256, 128))
    return pl.pallas_call(
        _ada_kernel,
        grid=(N // tn,),
        in_specs=[pl.BlockSpec((R, D), lambda j: (0, 0)),
                  pl.BlockSpec((D, tn), lambda j: (0, j)),
                  pl.BlockSpec((1, tn), lambda j: (0, j))],
        out_specs=pl.BlockSpec((R, tn), lambda j: (0, j)),
        out_shape=jax.ShapeDtypeStruct((R, N), F32),
        compiler_params=_cparams("arbitrary"),
        name="ada_mod",
    )(c, w, b)


def _mod_spec(arr, tm):
    D = arr.shape[-1]
    if arr.shape[1] == 1:
        return pl.BlockSpec((None, 1, D), lambda g, i: (g, 0, 0))
    return pl.BlockSpec((None, tm, D), lambda g, i: (g, i, 0))


def _lnmod_kernel(x_ref, sc_ref, sh_ref, o_ref):
    y = _ln(x_ref[...]) * (1.0 + sc_ref[...]) + sh_ref[...]
    o_ref[...] = y.astype(o_ref.dtype)


def ln_mod(x, sc, sh):
    G, R, D = x.shape
    tm = _pick(R, (512, 256, 128, 64, 32, 16, 8))
    row = pl.BlockSpec((None, tm, D), lambda g, i: (g, i, 0))
    return pl.pallas_call(
        _lnmod_kernel,
        grid=(G, R // tm),
        in_specs=[row, _mod_spec(sc, tm), _mod_spec(sh, tm)],
        out_specs=row,
        out_shape=jax.ShapeDtypeStruct((G, R, D), BF16),
        compiler_params=_cparams("parallel", "parallel"),
        name="ln_mod",
    )(x, sc, sh)


def _mm_kernel(a_ref, w_ref, o_ref):
    o_ref[...] = _dot(a_ref[...], w_ref[...]).astype(o_ref.dtype)


def matmul(a, w, out_dtype=F32):
    M, K = a.shape
    N = w.shape[1]
    tm = _pick(M, (1024, 512, 256, 128, 64, 32, 16, 8))
    tn = _pick(N, (1024, 512, 256, 128))
    return pl.pallas_call(
        _mm_kernel,
        grid=(M // tm, N // tn),
        in_specs=[pl.BlockSpec((tm, K), lambda i, j: (i, 0)),
                  pl.BlockSpec((K, tn), lambda i, j: (0, j))],
        out_specs=pl.BlockSpec((tm, tn), lambda i, j: (i, j)),
        out_shape=jax.ShapeDtypeStruct((M, N), out_dtype),
        compiler_params=_cparams("parallel", "parallel"),
        name="matmul",
    )(a, w)


def _rot(x, cos, sin):
    half = x.shape[-1] // 2
    x1 = x[:, :half]
    x2 = x[:, half:]
    return jnp.concatenate([x1 * cos - x2 * sin, x1 * sin + x2 * cos], axis=-1)


def _retention_kernel(q_ref, k_ref, v_ref, g_ref, cos_ref, sin_ref, intra_ref, kdec_ref, qdec_ref,
                      gch_ref, gnw_ref, s0_ref, o_ref, sfin_ref, s_scr, *, chunk, n_sub, q_scale):
    i = pl.program_id(2)

    @pl.when(i == 0)
    def _():
        s_scr[...] = s0_ref[...]

    intra = intra_ref[...]
    kdec = kdec_ref[...]
    qdec = qdec_ref[...]
    gch = gch_ref[...]
    gnw = gnw_ref[...]
    for c in range(n_sub):
        rows = pl.ds(c * chunk, chunk)
        cos = cos_ref[rows, :]
        sin = sin_ref[rows, :]
        q = _rot(q_ref[rows, :], cos, sin) * q_scale
        k = _rot(k_ref[rows, :], cos, sin)
        v = v_ref[rows, :].astype(BF16)
        s_prev = s_scr[...]
        scores = lax.dot_general(q.astype(BF16), k.astype(BF16), (((1,), (1,)), ((), ())),
                                 preferred_element_type=F32) * intra
        o = _dot(scores.astype(BF16), v) + _dot((q * qdec).astype(BF16), s_prev.astype(BF16))
        kv = lax.dot_general((k * kdec).astype(BF16), v, (((0,), (0,)), ((), ())),
                             preferred_element_type=F32)
        s_scr[...] = gch * s_prev + kv
        on = _ln(o) * gnw
        g = g_ref[rows, :]
        o_ref[rows, :] = (g * _sigmoid(g) * on).astype(o_ref.dtype)

    @pl.when(i == pl.num_programs(2) - 1)
    def _():
        sfin_ref[...] = s_scr[...]


def retention_mixer(proj, pos, s0, gn_w, n_heads, d_ret):
    B, T, _ = proj.shape
    dk = d_ret // n_heads
    dv = s0.shape[-1]
    chunk = min(T, CHUNK)
    n_chunks = T // chunk
    n_sub = _pick(n_chunks, (8, 4, 2, 1))
    tb = n_sub * chunk
    half = dk // 2
    inv = ROPE_BASE ** (-jnp.arange(half, dtype=F32) / half)
    ang = pos.astype(F32)[:, None] * inv[None, :]
    cos, sin = jnp.cos(ang), jnp.sin(ang)
    log_g = jnp.log1p(-jnp.exp2(-5.0 - jnp.arange(n_heads, dtype=F32)))
    ci = jnp.arange(chunk, dtype=F32)
    intra = jnp.exp(jnp.abs(ci[:, None] - ci[None, :])[None] * log_g[:, None, None])
    kdec = jnp.broadcast_to(jnp.exp((chunk - 1 - ci)[None, :] * log_g[:, None])[:, :, None],
                            (n_heads, chunk, dk))
    qdec = jnp.broadcast_to(jnp.exp((ci + 1.0)[None, :] * log_g[:, None])[:, :, None],
                            (n_heads, chunk, dk))
    gch = jnp.broadcast_to(jnp.exp(chunk * log_g)[:, None, None], (n_heads, 1, dv))

    def col(group):
        return pl.BlockSpec((None, tb, dk), lambda b, h, i: (b, i, group * n_heads + h))

    tab = pl.BlockSpec((tb, half), lambda b, h, i: (i, 0))
    per_head = lambda r, c: pl.BlockSpec((None, r, c), lambda b, h, i: (h, 0, 0))
    state = pl.BlockSpec((None, None, dk, dv), lambda b, h, i: (b, h, 0, 0))
    kern = functools.partial(_retention_kernel, chunk=chunk, n_sub=n_sub, q_scale=dk ** -0.5)
    return pl.pallas_call(
        kern,
        grid=(B, n_heads, T // tb),
        in_specs=[col(0), col(1), col(2), col(3), tab, tab,
                  per_head(chunk, chunk), per_head(chunk, dk), per_head(chunk, dk), per_head(1, dv),
                  pl.BlockSpec((1, dv), lambda b, h, i: (0, h)), state],
        out_specs=[pl.BlockSpec((None, tb, dv), lambda b, h, i: (b, i, h)), state],
        out_shape=[jax.ShapeDtypeStruct((B, T, n_heads * dv), BF16),
                   jax.ShapeDtypeStruct(s0.shape, F32)],
        scratch_shapes=[pltpu.VMEM((dk, dv), F32)],
        compiler_params=_cparams("parallel", "parallel", "arbitrary"),
        name="retention",
    )(proj, proj, proj, proj, cos, sin, intra, kdec, qdec, gch, gn_w, s0)


def _softplus(z):
    return jnp.maximum(z, 0.0) + jnp.log1p(jnp.exp(-jnp.abs(z)))


def _expm1(x):
    e = jnp.exp(x)
    em1 = e - 1.0
    return jnp.where(e == 1.0, x, jnp.where(em1 == -1.0, -1.0, em1 * x / jnp.log(e)))


def _gelu_tanh(x):
    return 0.5 * x * (1.0 + jnp.tanh(math.sqrt(2.0 / math.pi) * (x + 0.044715 * (x * x * x))))


def _rglru_kernel(x_ref, g_ref, cw_ref, cb_ref, wa_ref, ba_ref, wx_ref, bx_ref, lam_ref, h0_ref, c0_ref,
                  o_ref, hn_ref, cn_ref, xp_scr, a_scr, u_scr, h_scr, hc_scr, *, tb, conv_w, n_blocks):
    i = pl.program_id(1)
    hist = conv_w - 1
    lo = CONV_PAD - hist

    @pl.when(i == 0)
    def _():
        xp_scr[lo:CONV_PAD, :] = c0_ref[...]
        hc_scr[...] = h0_ref[...]

    @pl.when(i > 0)
    def _():
        xp_scr[lo:CONV_PAD, :] = xp_scr[tb + lo:tb + CONV_PAD, :]

    xp_scr[CONV_PAD:CONV_PAD + tb, :] = x_ref[...]
    xc = cb_ref[...] + xp_scr[lo:lo + tb, :] * cw_ref[0:1, :]
    for j in range(1, conv_w):
        xc = xc + xp_scr[lo + j:lo + j + tb, :] * cw_ref[j:j + 1, :]

    sp = RGLRU_C * _softplus(-lam_ref[...])
    bw = xc.shape[1] // n_blocks
    for n in range(n_blocks):
        cols = slice(n * bw, (n + 1) * bw)
        xb = xc[:, cols]
        xb16 = xb.astype(BF16)
        r = _sigmoid(_dot(xb16, wa_ref[n]) + ba_ref[:, cols])
        ig = _sigmoid(_dot(xb16, wx_ref[n]) + bx_ref[:, cols])
        log_a = -(r * sp[:, cols])
        a_scr[:, cols] = jnp.exp(log_a)
        u_scr[:, cols] = jnp.sqrt(-_expm1(2.0 * log_a)) * (ig * xb)

    def step(t, h):
        h = a_scr[pl.ds(t, 1), :] * h + u_scr[pl.ds(t, 1), :]
        h_scr[pl.ds(t, 1), :] = h
        return h

    h_last = lax.fori_loop(0, tb, step, hc_scr[...], unroll=8)
    hc_scr[...] = h_last
    o_ref[...] = (_gelu_tanh(g_ref[...]) * h_scr[...]).astype(o_ref.dtype)

    @pl.when(i == pl.num_programs(1) - 1)
    def _():
        hn_ref[...] = h_last
        cn_ref[...] = xp_scr[tb + lo:tb + CONV_PAD, :]


def rglru_mixer(proj, x_group, h0, conv0, conv_w, conv_b, w_a, b_a, w_x, b_x, lam):
    B, T, _ = proj.shape
    d_rnn = h0.shape[-1]
    cw = conv_w.shape[0]
    n_blocks = w_a.shape[0]
    tb = _pick(T, (256, 128, 64, 32, 16, 8))
    kern = functools.partial(_rglru_kernel, tb=tb, conv_w=cw, n_blocks=n_blocks)
    full = lambda *s: pl.BlockSpec(s, lambda b, i: (0,) * len(s))
    return pl.pallas_call(
        kern,
        grid=(B, T // tb),
        in_specs=[pl.BlockSpec((None, tb, d_rnn), lambda b, i: (b, i, x_group)),
                  pl.BlockSpec((None, tb, d_rnn), lambda b, i: (b, i, x_group + 1)),
                  full(cw, d_rnn), full(1, d_rnn),
                  full(*w_a.shape), full(1, d_rnn), full(*w_x.shape), full(1, d_rnn), full(1, d_rnn),
                  pl.BlockSpec((None, 1, d_rnn), lambda b, i: (b, 0, 0)),
                  pl.BlockSpec((None, cw - 1, d_rnn), lambda b, i: (b, 0, 0))],
        out_specs=[pl.BlockSpec((None, tb, d_rnn), lambda b, i: (b, i, 0)),
                   pl.BlockSpec((None, 1, d_rnn), lambda b, i: (b, 0, 0)),
                   pl.BlockSpec((None, cw - 1, d_rnn), lambda b, i: (b, 0, 0))],
        out_shape=[jax.ShapeDtypeStruct((B, T, d_rnn), BF16),
                   jax.ShapeDtypeStruct((B, 1, d_rnn), F32),
                   jax.ShapeDtypeStruct((B, cw - 1, d_rnn), F32)],
        scratch_shapes=[pltpu.VMEM((tb + CONV_PAD, d_rnn), F32), pltpu.VMEM((tb, d_rnn), F32),
                        pltpu.VMEM((tb, d_rnn), F32), pltpu.VMEM((tb, d_rnn), F32),
                        pltpu.VMEM((1, d_rnn), F32)],
        compiler_params=_cparams("parallel", "arbitrary"),
        name="rglru",
    )(proj, proj, conv_w, conv_b, w_a, b_a, w_x, b_x, lam, h0, conv0)


def _merge_kernel(pr_ref, pn_ref, wr_ref, wn_ref, g0_ref, g1_ref, o_ref):
    yr = _dot(pr_ref[...], wr_ref[...])
    yn = _dot(pn_ref[...], wn_ref[...])
    o_ref[...] = (_sigmoid(g0_ref[...]) * yr + _sigmoid(g1_ref[...]) * yn).astype(o_ref.dtype)


def merge_branches(pre_ret, pre_rnn, w_ret, w_rnn, proj, gate_off):
    M, kr = pre_ret.shape
    kn = pre_rnn.shape[1]
    D = w_ret.shape[1]
    tm = _pick(M, (1024, 512, 256, 128, 64, 32, 16, 8))
    tn = _pick(math.gcd(D, gate_off), (512, 256, 128))
    g0 = gate_off // tn
    g1 = (gate_off + D) // tn
    return pl.pallas_call(
        _merge_kernel,
        grid=(M // tm, D // tn),
        in_specs=[pl.BlockSpec((tm, kr), lambda i, j: (i, 0)),
                  pl.BlockSpec((tm, kn), lambda i, j: (i, 0)),
                  pl.BlockSpec((kr, tn), lambda i, j: (0, j)),
                  pl.BlockSpec((kn, tn), lambda i, j: (0, j)),
                  pl.BlockSpec((tm, tn), lambda i, j: (i, g0 + j)),
                  pl.BlockSpec((tm, tn), lambda i, j: (i, g1 + j))],
        out_specs=pl.BlockSpec((tm, tn), lambda i, j: (i, j)),
        out_shape=jax.ShapeDtypeStruct((M, D), BF16),
        compiler_params=_cparams("parallel", "parallel"),
        name="merge_branches",
    )(pre_ret, pre_rnn, w_ret, w_rnn, proj, proj)


def _pack_pair(x):
    half = x.shape[-1] // 2
    lo = lax.bitcast_convert_type(x[:, :half].astype(BF16).astype(F32), jnp.uint32) >> 16
    hi = lax.bitcast_convert_type(x[:, half:].astype(BF16).astype(F32), jnp.uint32) & jnp.uint32(0xFFFF0000)
    return lo | hi


def _unpack_pair(w):
    lo = lax.bitcast_convert_type(w << 16, F32)
    hi = lax.bitcast_convert_type(w & jnp.uint32(0xFFFF0000), F32)
    return lo, hi


def _post1_kernel(x_ref, mix_ref, g1_ref, sc_ref, sh_ref, w_ref, b_ref, x1_ref, h2_ref, *, alpha):
    x1 = _ln(alpha * x_ref[...] + g1_ref[...] * mix_ref[...]) * w_ref[...] + b_ref[...]
    x1_ref[...] = x1
    h2_ref[...] = _pack_pair(_ln(x1) * (1.0 + sc_ref[...]) + sh_ref[...])


def post_mixer(x, mix, g1, sc2, sh2, ln_w, ln_b, alpha):
    G, R, D = x.shape
    tm = _pick(R, (256, 128, 64, 32, 16, 8))
    row = pl.BlockSpec((None, tm, D), lambda g, i: (g, i, 0))
    vec = pl.BlockSpec((1, D), lambda g, i: (0, 0))
    return pl.pallas_call(
        functools.partial(_post1_kernel, alpha=alpha),
        grid=(G, R // tm),
        in_specs=[row, row, _mod_spec(g1, tm), _mod_spec(sc2, tm), _mod_spec(sh2, tm), vec, vec],
        out_specs=[row, pl.BlockSpec((None, tm, D // 2), lambda g, i: (g, i, 0))],
        out_shape=[jax.ShapeDtypeStruct((G, R, D), F32), jax.ShapeDtypeStruct((G, R, D // 2), jnp.uint32)],
        compiler_params=_cparams("parallel", "parallel"),
        name="post_mixer",
    )(x, mix, g1, sc2, sh2, ln_w, ln_b)


def _router_kernel(h_ref, wlo_ref, whi_ref, b_ref, idx_ref, wt_ref, rank_ref, cnt_ref, carry_scr, *, n_experts):
    tm = h_ref.shape[0]
    gsz = n_experts // N_GROUPS
    neg = -jnp.inf

    @pl.when(pl.program_id(0) == 0)
    def _():
        carry_scr[...] = jnp.zeros_like(carry_scr)

    h_lo, h_hi = _unpack_pair(h_ref[...])
    s = _sigmoid(_dot(h_lo.astype(BF16), wlo_ref[...]) + _dot(h_hi.astype(BF16), whi_ref[...]))
    sb = s + b_ref[...]
    lane = lax.broadcasted_iota(jnp.int32, (tm, n_experts), 1).astype(F32)
    grp = jnp.zeros((tm, n_experts), F32)
    for g in range(1, N_GROUPS):
        grp = grp + (lane >= float(g * gsz)).astype(F32)
    lane_o = lax.broadcasted_iota(jnp.int32, (tm, LANES), 1).astype(F32)

    def argmax_first(v, ids, big):
        m = jnp.max(v, axis=-1, keepdims=True)
        first = jnp.min(jnp.where(v == m, ids, float(big)), axis=-1, keepdims=True)
        return m, first

    gscore = jnp.full((tm, LANES), neg, F32)
    for g in range(N_GROUPS):
        vg = jnp.where(grp == g, sb, neg)
        m1, i1 = argmax_first(vg, lane, n_experts)
        m2 = jnp.max(jnp.where(lane == i1, neg, vg), axis=-1, keepdims=True)
        gscore = jnp.where(lane_o == g, m1 + m2, gscore)
    keep = jnp.zeros((tm, n_experts), jnp.bool_)
    for _ in range(TOPK_GROUPS):
        _, gi = argmax_first(gscore, lane_o, LANES)
        gscore = jnp.where(lane_o == gi, neg, gscore)
        keep = jnp.logical_or(keep, grp == gi)
    sel = jnp.where(keep, sb, neg)
    idx_o = jnp.zeros((tm, LANES), F32)
    wt_o = jnp.zeros((tm, LANES), F32)
    chosen = []
    taken = jnp.zeros((tm, n_experts), F32)
    for k in range(TOP_K):
        _, ei = argmax_first(sel, lane, n_experts)
        hit = lane == ei
        sel = jnp.where(hit, neg, sel)
        wk = jnp.sum(jnp.where(hit, s, 0.0), axis=-1, keepdims=True)
        idx_o = jnp.where(lane_o == k, ei, idx_o)
        wt_o = jnp.where(lane_o == k, wk, wt_o)
        taken = jnp.where(hit, 1.0, taken)
        chosen.append(ei)
    wt_o = wt_o / jnp.sum(wt_o, axis=-1, keepdims=True) * ROUTED_SCALE
    idx_ref[...] = idx_o.astype(jnp.int32)
    wt_ref[...] = wt_o
    ri = lax.broadcasted_iota(jnp.int32, (tm, tm), 0)
    ci = lax.broadcasted_iota(jnp.int32, (tm, tm), 1)
    before = _dot((ri > ci).astype(BF16), taken.astype(BF16)) + carry_scr[...]
    rank_o = jnp.zeros((tm, LANES), F32)
    for k in range(TOP_K):
        rk = jnp.sum(jnp.where(lane == chosen[k], before, 0.0), axis=-1, keepdims=True)
        rank_o = jnp.where(lane_o == k, rk, rank_o)
    rank_ref[...] = rank_o.astype(jnp.int32)
    carry_scr[...] = carry_scr[...] + jnp.sum(taken, axis=0, keepdims=True)
    cnt_ref[...] = carry_scr[...]


def router(h2p, w_router, bias):
    M, H = h2p.shape
    E = w_router.shape[1]
    tm = _pick(M, (256, 128, 64, 32, 16, 8))
    out = pl.BlockSpec((tm, LANES), lambda i: (i, 0))
    return pl.pallas_call(
        functools.partial(_router_kernel, n_experts=E),
        grid=(M // tm,),
        in_specs=[pl.BlockSpec((tm, H), lambda i: (i, 0)),
                  pl.BlockSpec((H, E), lambda i: (0, 0)),
                  pl.BlockSpec((H, E), lambda i: (1, 0)),
                  pl.BlockSpec((1, E), lambda i: (0, 0))],
        out_specs=[out, out, out, pl.BlockSpec((1, E), lambda i: (0, 0))],
        out_shape=[jax.ShapeDtypeStruct((M, LANES), jnp.int32), jax.ShapeDtypeStruct((M, LANES), F32),
                   jax.ShapeDtypeStruct((M, LANES), jnp.int32), jax.ShapeDtypeStruct((1, E), F32)],
        scratch_shapes=[pltpu.VMEM((1, E), F32)],
        compiler_params=_cparams("arbitrary"),
        name="router",
    )(h2p, w_router, w_router, bias)


def _row_copy(src, src_row, dst, dst_row, sem):
    return pltpu.make_async_copy(src.at[pl.ds(src_row, 1)], dst.at[pl.ds(dst_row, 1)], sem)


def _pos_copy(pos_hbm, pos_smem, isem, tile, s, n_rows):
    start = pl.multiple_of(s * n_rows, n_rows)
    return pltpu.make_async_copy(pos_hbm.at[tile], pos_smem.at[pl.ds(start, n_rows)], isem.at[s])


def _pad_chunks(rows):
    n, out = rows // 2, []
    while n >= SUBLANES:
        out.append(n)
        n //= 2
    return out


def _dispatch_kernel(pstart_ref, count_ref, padded_ref, nused_ref, pos_hbm, h_ref, xs_hbm,
                     pos_smem, xbuf, zbuf, isem, dsem, zsem, *, tm, n_experts, bm):
    i = pl.program_id(0)
    nt = pl.num_programs(0)
    slot = i % 2
    nxt = 1 - slot
    n_rows = tm * TOP_K

    def pad_fill(wait):
        def body(e, c):
            npad = padded_ref[e] - count_ref[e]
            off = pstart_ref[e] + count_ref[e]
            n_single = (-count_ref[e]) & (SUBLANES - 1)
            for j in range(SUBLANES - 1):
                cp = _row_copy(zbuf, 0, xs_hbm, off + j, zsem)

                @pl.when(j < n_single)
                def _():
                    cp.wait() if wait else cp.start()

            off = pl.multiple_of(off + n_single, SUBLANES)
            n_tiled = npad - n_single
            for n in _pad_chunks(bm):
                cp = pltpu.make_async_copy(zbuf.at[pl.ds(0, n)], xs_hbm.at[pl.ds(off, n)], zsem)

                @pl.when((n_tiled & n) != 0)
                def _():
                    cp.wait() if wait else cp.start()

                off = pl.multiple_of(off + (n_tiled & n), SUBLANES)
            return c
        lax.fori_loop(0, n_experts, body, 0)

        def tail(b, c):
            for part in range(2):
                row0 = pl.multiple_of(b * bm + part * (bm // 2), SUBLANES)
                cp = pltpu.make_async_copy(zbuf, xs_hbm.at[pl.ds(row0, bm // 2)], zsem)
                cp.wait() if wait else cp.start()
            return c
        lax.fori_loop(nused_ref[0], xs_hbm.shape[0] // bm, tail, 0)

    @pl.when(i == 0)
    def _():
        zbuf[...] = jnp.zeros_like(zbuf)
        pad_fill(False)
        _pos_copy(pos_hbm, pos_smem, isem, 0, 0, n_rows).start()
        _pos_copy(pos_hbm, pos_smem, isem, 0, 0, n_rows).wait()
        pad_fill(True)

    @pl.when(i + 1 < nt)
    def _():
        _pos_copy(pos_hbm, pos_smem, isem, i + 1, nxt, n_rows).start()

    xbuf[slot] = h_ref[...]
    base = slot * n_rows

    def issue(r, c):
        for k in range(TOP_K):
            _row_copy(xbuf.at[slot], r, xs_hbm, pos_smem[base + r * TOP_K + k], dsem.at[slot]).start()
        return c
    lax.fori_loop(0, tm, issue, 0)

    def wait_all(s):
        def body(r, c):
            _row_copy(xbuf.at[s], 0, xs_hbm, 0, dsem.at[s]).wait()
            return c
        lax.fori_loop(0, n_rows, body, 0, unroll=8)

    @pl.when(i > 0)
    def _():
        wait_all(nxt)

    @pl.when(i + 1 < nt)
    def _():
        _pos_copy(pos_hbm, pos_smem, isem, i + 1, nxt, n_rows).wait()

    @pl.when(i == nt - 1)
    def _():
        wait_all(slot)


def dispatch_tokens(h2p, pos_tiles, pstart, counts, padded, n_used, n_slots, bm):
    M, H = h2p.shape
    nt, n_rows = pos_tiles.shape
    tm = n_rows // TOP_K
    any_spec = pl.BlockSpec(memory_space=pl.ANY)
    grid_spec = pltpu.PrefetchScalarGridSpec(
        num_scalar_prefetch=4,
        grid=(nt,),
        in_specs=[any_spec, pl.BlockSpec((tm, H), lambda i, *_: (i, 0))],
        out_specs=any_spec,
        scratch_shapes=[pltpu.SMEM((2 * n_rows,), jnp.int32), pltpu.VMEM((2, tm, H), jnp.uint32),
                        pltpu.VMEM((bm // 2, H), jnp.uint32),
                        pltpu.SemaphoreType.DMA((2,)), pltpu.SemaphoreType.DMA((2,)),
                        pltpu.SemaphoreType.DMA],
    )
    return pl.pallas_call(
        functools.partial(_dispatch_kernel, tm=tm, n_experts=pstart.shape[0], bm=bm),
        grid_spec=grid_spec,
        out_shape=jax.ShapeDtypeStruct((n_slots, H), jnp.uint32),
        compiler_params=_cparams("arbitrary"),
        name="dispatch",
    )(pstart, counts, padded, n_used, pos_tiles, h2p)


def _experts_kernel(blk_e_ref, first_ref, next_ref, nused_ref, xs_ref, w1_hbm, w3_hbm, w2_hbm, y_ref,
                    st1, st3, st2, wb1, wb3, wb2, wsem):
    i = pl.program_id(0)
    half = xs_ref.shape[1]

    def weight_copies(e):
        return (pltpu.make_async_copy(w1_hbm.at[e], st1, wsem.at[0]),
                pltpu.make_async_copy(w3_hbm.at[e], st3, wsem.at[1]),
                pltpu.make_async_copy(w2_hbm.at[e], st2, wsem.at[2]))

    @pl.when(i == 0)
    def _():
        for cp in weight_copies(blk_e_ref[0]):
            cp.start()

    @pl.when(first_ref[i] == 1)
    def _():
        for cp in weight_copies(blk_e_ref[i]):
            cp.wait()
        wb1[...] = st1[...].astype(BF16)
        wb3[...] = st3[...].astype(BF16)
        wb2[...] = st2[...].astype(BF16)

        @pl.when(next_ref[i] >= 0)
        def _():
            for cp in weight_copies(next_ref[i]):
                cp.start()

    @pl.when(i < nused_ref[0])
    def _():
        x_lo, x_hi = _unpack_pair(xs_ref[...])
        x_lo = x_lo.astype(BF16)
        x_hi = x_hi.astype(BF16)
        h1 = _dot(x_lo, wb1[0:half, :]) + _dot(x_hi, wb1[half:2 * half, :])
        h3 = _dot(x_lo, wb3[0:half, :]) + _dot(x_hi, wb3[half:2 * half, :])
        hh = (h1 * _sigmoid(h1) * h3).astype(BF16)
        y_ref[...] = _pack_pair(_dot(hh, wb2[...]))

    @pl.when(i >= nused_ref[0])
    def _():
        y_ref[...] = jnp.zeros_like(y_ref)


def routed_expert_blocks(xs, blk_e, first, next_e, n_used, w1, w3, w2, bm):
    n_slots, H = xs.shape
    nb = n_slots // bm
    _, D, F = w1.shape
    any_spec = pl.BlockSpec(memory_space=pl.ANY)
    grid_spec = pltpu.PrefetchScalarGridSpec(
        num_scalar_prefetch=4,
        grid=(nb,),
        in_specs=[pl.BlockSpec((bm, H), lambda i, be, fi, ne, nu: (jnp.minimum(i, nu[0] - 1), 0)),
                  any_spec, any_spec, any_spec],
        out_specs=pl.BlockSpec((bm, H), lambda i, be, fi, ne, nu: (i, 0)),
        scratch_shapes=[pltpu.VMEM((D, F), F32), pltpu.VMEM((D, F), F32), pltpu.VMEM((F, D), F32),
                        pltpu.VMEM((D, F), BF16), pltpu.VMEM((D, F), BF16), pltpu.VMEM((F, D), BF16),
                        pltpu.SemaphoreType.DMA((3,))],
    )
    return pl.pallas_call(
        _experts_kernel,
        grid_spec=grid_spec,
        out_shape=jax.ShapeDtypeStruct((n_slots, H), jnp.uint32),
        compiler_params=_cparams("arbitrary"),
        name="routed_experts",
    )(blk_e, first, next_e, n_used, xs, w1, w3, w2)


def _final_kernel(pos_hbm, y_hbm, x1_ref, h2_ref, wt_ref, g2_ref, ws1_ref, ws3_ref, ws2_ref,
                  w_ref, b_ref, o_ref, ybuf, pos_smem, gsem, isem, *, tm, alpha, tile_off):
    g = pl.program_id(0)
    i = pl.program_id(1)
    nt = pl.num_programs(1)
    step = g * nt + i
    nsteps = pl.num_programs(0) * nt
    slot = step % 2
    nxt = 1 - slot
    n_rows = tm * TOP_K
    half = h2_ref.shape[1]

    def pos_copy(st, s):
        return _pos_copy(pos_hbm, pos_smem, isem, tile_off + st, s, n_rows)

    def issue_rows(s):
        base = s * n_rows

        def body(r, c):
            for k in range(TOP_K):
                _row_copy(y_hbm, pos_smem[base + r * TOP_K + k], ybuf.at[s], k * tm + r, gsem.at[s]).start()
            return c
        lax.fori_loop(0, tm, body, 0)

    def wait_rows(s):
        def body(r, c):
            _row_copy(y_hbm, 0, ybuf.at[s], r, gsem.at[s]).wait()
            return c
        lax.fori_loop(0, n_rows, body, 0, unroll=8)

    @pl.when(step == 0)
    def _():
        pos_copy(0, 0).start()
        pos_copy(0, 0).wait()
        issue_rows(0)

        @pl.when(nsteps > 1)
        def _():
            pos_copy(1, 1).start()

    @pl.when(step + 1 < nsteps)
    def _():
        pos_copy(step + 1, nxt).wait()
        issue_rows(nxt)

    h_lo, h_hi = _unpack_pair(h2_ref[...])
    h_lo = h_lo.astype(BF16)
    h_hi = h_hi.astype(BF16)
    a1 = _dot(h_lo, ws1_ref[0:half, :]) + _dot(h_hi, ws1_ref[half:2 * half, :])
    a3 = _dot(h_lo, ws3_ref[0:half, :]) + _dot(h_hi, ws3_ref[half:2 * half, :])
    shared = _dot((a1 * _sigmoid(a1) * a3).astype(BF16), ws2_ref[...])
    ffn_lo = shared[:, :half]
    ffn_hi = shared[:, half:]

    wait_rows(slot)

    @pl.when(step + 2 < nsteps)
    def _():
        pos_copy(step + 2, slot).start()

    wt = wt_ref[...]
    for k in range(TOP_K):
        y_lo, y_hi = _unpack_pair(ybuf[slot, k * tm:(k + 1) * tm, :])
        wk = wt[:, k:k + 1]
        ffn_lo = ffn_lo + wk * y_lo
        ffn_hi = ffn_hi + wk * y_hi
    ffn = jnp.concatenate([ffn_lo, ffn_hi], axis=-1)
    o_ref[...] = _ln(alpha * x1_ref[...] + g2_ref[...] * ffn) * w_ref[...] + b_ref[...]


def combine_and_finish(x1, h2p, wts, pos_tiles, tile_off, y, g2, ws1, ws3, ws2, ln_w, ln_b, alpha):
    G, R, D = x1.shape
    H = D // 2
    n_rows = pos_tiles.shape[1]
    tm = n_rows // TOP_K
    F = ws1.shape[1]
    any_spec = pl.BlockSpec(memory_space=pl.ANY)
    row = lambda w: pl.BlockSpec((None, tm, w), lambda g, i: (g, i, 0))
    const = lambda r, c: pl.BlockSpec((r, c), lambda g, i: (0, 0))
    kern = functools.partial(_final_kernel, tm=tm, alpha=alpha, tile_off=tile_off)
    return pl.pallas_call(
        kern,
        grid=(G, R // tm),
        in_specs=[any_spec, any_spec, row(D), row(H), row(LANES), _mod_spec(g2, tm),
                  const(D, F), const(D, F), const(F, D), const(1, D), const(1, D)],
        out_specs=row(D),
        out_shape=jax.ShapeDtypeStruct((G, R, D), F32),
        scratch_shapes=[pltpu.VMEM((2, n_rows, H), jnp.uint32), pltpu.SMEM((2 * n_rows,), jnp.int32),
                        pltpu.SemaphoreType.DMA((2,)), pltpu.SemaphoreType.DMA((2,))],
        compiler_params=_cparams("arbitrary", "arbitrary"),
        name="combine_finish",
    )(pos_tiles, y, x1, h2p, wts, g2, ws1, ws3, ws2, ln_w, ln_b)


def _slots_kernel(idx_ref, rank_ref, pstart_ref, pos_ref):
    tm = idx_ref.shape[0]
    E = pstart_ref.shape[1]
    lane = lax.broadcasted_iota(jnp.int32, (tm, E), 1)
    lane_o = lax.broadcasted_iota(jnp.int32, (tm, LANES), 1)
    idx = idx_ref[...]
    ps = pstart_ref[...]
    start = jnp.zeros((tm, LANES), F32)
    for k in range(TOP_K):
        sk = jnp.sum(jnp.where(lane == idx[:, k:k + 1], ps, 0.0), axis=-1, keepdims=True)
        start = jnp.where(lane_o == k, sk, start)
    pos_ref[...] = start.astype(jnp.int32) + rank_ref[...]


def assignment_slots(idx, rank, pstart):
    M = idx.shape[0]
    E = pstart.shape[1]
    tm = _pick(M, (256, 128, 64, 32, 16, 8))
    blk = pl.BlockSpec((tm, LANES), lambda i: (i, 0))
    return pl.pallas_call(
        _slots_kernel,
        grid=(M // tm,),
        in_specs=[blk, blk, pl.BlockSpec((1, E), lambda i: (0, 0))],
        out_specs=blk,
        out_shape=jax.ShapeDtypeStruct((M, LANES), jnp.int32),
        compiler_params=_cparams("parallel"),
        name="assignment_slots",
    )(idx, rank, pstart)


def _block_tables(counts, n_blocks, bm):
    E = counts.shape[0]
    padded = (counts + bm - 1) // bm * bm
    pend = jnp.cumsum(padded)
    pstart = pend - padded
    n_used = pend[-1] // bm
    blk = jnp.arange(n_blocks, dtype=jnp.int32)
    blk_e = jnp.minimum(jnp.searchsorted(pend, blk * bm, side="right"), E - 1).astype(jnp.int32)
    prev_e = jnp.concatenate([jnp.full((1,), -1, jnp.int32), blk_e[:-1]])
    first = ((blk_e != prev_e) & (blk < n_used)).astype(jnp.int32)
    ids = jnp.arange(E, dtype=jnp.int32)
    active = jnp.where(counts > 0, ids, E)
    later = lax.cummin(active[::-1])[::-1]
    nxt = jnp.concatenate([later[1:], jnp.full((1,), E, jnp.int32)])
    nxt = jnp.where(nxt >= E, -1, nxt)
    next_e = nxt[blk_e].astype(jnp.int32)
    i32 = lambda v: v.astype(jnp.int32)
    return i32(pstart), i32(padded), blk_e, first, next_e, i32(n_used).reshape(1)


def _assignment_tiles(v, tm):
    M = v.shape[0]
    return v[:, :TOP_K].reshape(M // tm, tm * TOP_K)


def _stream_mixers(x, mods, pos, s_ret, h_rnn, conv_buf, lw, alpha):
    B, T, D = x.shape
    n_heads, d_ret = lw["n_heads"], lw["d_ret"]
    d_rnn = lw["d_rnn"]
    per_row = mods["per_row"]
    xg = x.reshape(1, B * T, D) if per_row else x
    h = ln_mod(xg, mods["sc1"], mods["sh1"]).reshape(B * T, D)
    proj = matmul(h, lw["w_in"]).reshape(B, T, -1)
    pre_ret, s_new = retention_mixer(proj, pos, s_ret, lw["ret_gn_w"], n_heads, d_ret)
    x_group = (4 * d_ret) // d_rnn
    pre_rnn, h_new, conv_new = rglru_mixer(proj, x_group, h_rnn[:, None, :], conv_buf, lw["conv_w"],
                                           lw["conv_b"], lw["w_a"], lw["b_a"], lw["w_x"], lw["b_x"],
                                           lw["rg_lambda"])
    mixin = merge_branches(pre_ret.reshape(B * T, d_ret), pre_rnn.reshape(B * T, d_rnn),
                           lw["w_br_ret"], lw["w_br_rnn"], proj.reshape(B * T, -1), 4 * d_ret + 2 * d_rnn)
    mix = matmul(mixin, lw["w_o"])
    x1, h2 = post_mixer(xg, mix.reshape(xg.shape), mods["g1"], mods["sc2"], mods["sh2"],
                        lw["ln1_w"], lw["ln1_b"], alpha)
    return x1, h2, s_new, h_new[:, 0, :], conv_new


def _split_mods(mod, B, T, per_row):
    D = mod.shape[1] // 6
    parts = mod.reshape(B, 6, D)
    out = {"per_row": per_row}
    for n, name in enumerate(("sh1", "sc1", "g1", "sh2", "sc2", "g2")):
        p = parts[:, n, :]
        out[name] = jnp.repeat(p, T, axis=0)[None] if per_row else p[:, None, :]
    return out


def kernel(x_prompt, x_sample, state_ret, state_rglru, state_conv, c_prompt, c_sample, w_ada, b_ada, w_in, ret_gn_w, conv_w, conv_b, w_a, b_a, w_x, b_x, rg_lambda, w_br_ret, w_br_rnn, w_o, ln1_w, ln1_b, w_router, router_bias, w1, w3, w2, ws1, ws3, ws2, ln2_w, ln2_b):
    B, T, D = x_prompt.shape
    Bs, Ts, _ = x_sample.shape
    depth = w_ada.shape[0]
    n_heads, dk, dv = state_ret.shape[2:]
    d_ret = n_heads * dk
    d_rnn = state_rglru.shape[-1]
    cw = conv_w.shape[1]
    n_experts = w_router.shape[-1]
    alpha = (2.0 * depth) ** 0.25
    pos_p = jnp.arange(T, dtype=jnp.int32)
    pos_s = PAST_LEN + jnp.arange(Ts, dtype=jnp.int32)
    row = lambda v: v.reshape(1, -1)

    yp, ys = x_prompt, x_sample
    outs = [[] for _ in range(6)]
    for l in range(depth):
        lw = dict(n_heads=n_heads, d_ret=d_ret, d_rnn=d_rnn,
                  w_in=w_in[l].astype(BF16), ret_gn_w=row(ret_gn_w[l]), conv_w=conv_w[l], conv_b=row(conv_b[l]),
                  w_a=w_a[l].astype(BF16), b_a=row(b_a[l]), w_x=w_x[l].astype(BF16), b_x=row(b_x[l]),
                  rg_lambda=row(rg_lambda[l]), w_br_ret=w_br_ret[l].astype(BF16),
                  w_br_rnn=w_br_rnn[l].astype(BF16), w_o=w_o[l].astype(BF16),
                  ln1_w=row(ln1_w[l]), ln1_b=row(ln1_b[l]))
        mod = ada_mod(jnp.concatenate([c_prompt, c_sample], axis=0), w_ada[l], row(b_ada[l]))
        mods_p = _split_mods(mod[:B], B, T, per_row=False)
        mods_s = _split_mods(mod[B:], Bs, Ts, per_row=True)

        zero_ret = jnp.zeros((B, n_heads, dk, dv), F32)
        zero_h = jnp.zeros((B, d_rnn), F32)
        zero_conv = jnp.zeros((B, cw - 1, d_rnn), F32)
        x1p, h2p, r1, h1, c1 = _stream_mixers(yp, mods_p, pos_p, zero_ret, zero_h, zero_conv, lw, alpha)
        x1s, h2s, r2, hh2, c2 = _stream_mixers(ys, mods_s, pos_s, state_ret[l], state_rglru[l],
                                               state_conv[l], lw, alpha)

        Mp, Ms = B * T, Bs * Ts
        M = Mp + Ms
        bm = EXPERT_ROWS
        tm = math.gcd(math.gcd(T, Ms), TOKEN_TILE)
        h2_all = jnp.concatenate([h2p.reshape(Mp, D // 2), h2s.reshape(Ms, D // 2)], axis=0)
        idx_l, wt_l, rank_l, cnt = router(h2_all, w_router[l].astype(BF16), row(router_bias[l]))
        n_blocks = -(-(M * TOP_K) // bm) + n_experts
        pstart, padded, blk_e, first, next_e, n_used = _block_tables(cnt[0].astype(jnp.int32), n_blocks, bm)
        pos_t = _assignment_tiles(assignment_slots(idx_l, rank_l, row(pstart).astype(F32)), tm)
        xs = dispatch_tokens(h2_all, pos_t, pstart, cnt[0].astype(jnp.int32), padded, n_used, n_blocks * bm, bm)
        y_slots = routed_expert_blocks(xs, blk_e, first, next_e, n_used, w1[l], w3[l], w2[l], bm)
        ws = (ws1[l].astype(BF16), ws3[l].astype(BF16), ws2[l].astype(BF16))
        yp = combine_and_finish(x1p, h2p, wt_l[:Mp].reshape(B, T, LANES), pos_t, 0,
                                y_slots, mods_p["g2"], *ws, row(ln2_w[l]), row(ln2_b[l]), alpha)
        ys = combine_and_finish(x1s, h2s, wt_l[Mp:].reshape(1, Ms, LANES), pos_t, Mp // tm,
                                y_slots, mods_s["g2"], *ws, row(ln2_w[l]), row(ln2_b[l]), alpha)
        ys = ys.reshape(Bs, Ts, D)
        for lst, val in zip(outs, (r1, h1, c1, r2, hh2, c2)):
            lst.append(val)
    return (yp, ys) + tuple(jnp.stack(o) for o in outs)
```

```python
import functools
import math

import jax
import jax.numpy as jnp
from jax import lax
from jax.experimental import pallas as pl
from jax.experimental.pallas import tpu as pltpu

CHUNK = 64
TOP_K = 8
N_GROUPS = 8
TOPK_GROUPS = 4
ROUTED_SCALE = 2.5
RGLRU_C = 8.0
PAST_LEN = 4096
ROPE_BASE = 10000.0
LN_EPS = 1e-5

F32 = jnp.float32
BF16 = jnp.bfloat16

V7X_VMEM_LIMIT_BYTES = 56 * 1024 * 1024
LANES = 128
SUBLANES = 8
EXPERT_ROWS = 256
TOKEN_TILE = 128
CONV_PAD = 8


def _cparams(*sem):
    return pltpu.CompilerParams(dimension_semantics=sem, vmem_limit_bytes=V7X_VMEM_LIMIT_BYTES)


def _pick(n, cands):
    for c in cands:
        if n % c == 0:
            return c
    return n


def _ln(x):
    mu = jnp.mean(x, axis=-1, keepdims=True)
    xc = x - mu
    var = jnp.mean(xc * xc, axis=-1, keepdims=True)
    return xc * lax.rsqrt(var + LN_EPS)


def _sigmoid(x):
    return jax.nn.sigmoid(x)


def _dot(a, b):
    return jnp.dot(a, b, preferred_element_type=F32)


def _ada_kernel(c_ref, w_ref, b_ref, o_ref):
    c = c_ref[...]
    s = (c * _sigmoid(c)).astype(BF16)
    o_ref[...] = _dot(s, w_ref[...].astype(BF16)) + b_ref[...]


def ada_mod(c, w, b):
    R, D = c.shape
    N = w.shape[1]
    tn = _pick(N, (512, 256, 128))
    return pl.pallas_call(
        _ada_kernel,
        grid=(N // tn,),
        in_specs=[pl.BlockSpec((R, D), lambda j: (0, 0)),
                  pl.BlockSpec((D, tn), lambda j: (0, j)),
                  pl.BlockSpec((1, tn), lambda j: (0, j))],
        out_specs=pl.BlockSpec((R, tn), lambda j: (0, j)),
        out_shape=jax.ShapeDtypeStruct((R, N), F32),
        compiler_params=_cparams("arbitrary"),
        name="ada_mod",
    )(c, w, b)


def _mod_spec(arr, tm):
    D = arr.shape[-1]
    if arr.shape[1] == 1:
        return pl.BlockSpec((None, 1, D), lambda g, i: (g, 0, 0))
    return pl.BlockSpec((None, tm, D), lambda g, i: (g, i, 0))


def _lnmod_kernel(x_ref, sc_ref, sh_ref, o_ref):
    y = _ln(x_ref[...]) * (1.0 + sc_ref[...]) + sh_ref[...]
    o_ref[...] = y.astype(o_ref.dtype)


def ln_mod(x, sc, sh):
    G, R, D = x.shape
    tm = _pick(R, (512, 256, 128, 64, 32, 16, 8))
    row = pl.BlockSpec((None, tm, D), lambda g, i: (g, i, 0))
    return pl.pallas_call(
        _lnmod_kernel,
        grid=(G, R // tm),
        in_specs=[row, _mod_spec(sc, tm), _mod_spec(sh, tm)],
        out_specs=row,
        out_shape=jax.ShapeDtypeStruct((G, R, D), BF16),
        compiler_params=_cparams("parallel", "parallel"),
        name="ln_mod",
    )(x, sc, sh)


def _mm_kernel(a_ref, w_ref, o_ref):
    o_ref[...] = _dot(a_ref[...], w_ref[...]).astype(o_ref.dtype)


def matmul(a, w, out_dtype=F32):
    M, K = a.shape
    N = w.shape[1]
    tm = _pick(M, (1024, 512, 256, 128, 64, 32, 16, 8))
    tn = _pick(N, (1024, 512, 256, 128))
    return pl.pallas_call(
        _mm_kernel,
        grid=(M // tm, N // tn),
        in_specs=[pl.BlockSpec((tm, K), lambda i, j: (i, 0)),
                  pl.BlockSpec((K, tn), lambda i, j: (0, j))],
        out_specs=pl.BlockSpec((tm, tn), lambda i, j: (i, j)),
        out_shape=jax.ShapeDtypeStruct((M, N), out_dtype),
        compiler_params=_cparams("parallel", "parallel"),
        name="matmul",
    )(a, w)


def _rot(x, cos, sin):
    half = x.shape[-1] // 2
    x1 = x[:, :half]
    x2 = x[:, half:]
    return jnp.concatenate([x1 * cos - x2 * sin, x1 * sin + x2 * cos], axis=-1)


def _retention_kernel(q_ref, k_ref, v_ref, g_ref, cos_ref, sin_ref, intra_ref, kdec_ref, qdec_ref,
                      gch_ref, gnw_ref, s0_ref, o_ref, sfin_ref, s_scr, *, chunk, n_sub, q_scale):
    i = pl.program_id(2)

    @pl.when(i == 0)
    def _():
        s_scr[...] = s0_ref[...]

    intra = intra_ref[...]
    kdec = kdec_ref[...]
    qdec = qdec_ref[...]
    gch = gch_ref[...]
    gnw = gnw_ref[...]
    for c in range(n_sub):
        rows = pl.ds(c * chunk, chunk)
        cos = cos_ref[rows, :]
        sin = sin_ref[rows, :]
        q = _rot(q_ref[rows, :], cos, sin) * q_scale
        k = _rot(k_ref[rows, :], cos, sin)
        v = v_ref[rows, :].astype(BF16)
        s_prev = s_scr[...]
        scores = lax.dot_general(q.astype(BF16), k.astype(BF16), (((1,), (1,)), ((), ())),
                                 preferred_element_type=F32) * intra
        o = _dot(scores.astype(BF16), v) + _dot((q * qdec).astype(BF16), s_prev.astype(BF16))
        kv = lax.dot_general((k * kdec).astype(BF16), v, (((0,), (0,)), ((), ())),
                             preferred_element_type=F32)
        s_scr[...] = gch * s_prev + kv
        on = _ln(o) * gnw
        g = g_ref[rows, :]
        o_ref[rows, :] = (g * _sigmoid(g) * on).astype(o_ref.dtype)

    @pl.when(i == pl.num_programs(2) - 1)
    def _():
        sfin_ref[...] = s_scr[...]


def retention_mixer(proj, pos, s0, gn_w, n_heads, d_ret):
    B, T, _ = proj.shape
    dk = d_ret // n_heads
    dv = s0.shape[-1]
    chunk = min(T, CHUNK)
    n_chunks = T // chunk
    n_sub = _pick(n_chunks, (8, 4, 2, 1))
    tb = n_sub * chunk
    half = dk // 2
    inv = ROPE_BASE ** (-jnp.arange(half, dtype=F32) / half)
    ang = pos.astype(F32)[:, None] * inv[None, :]
    cos, sin = jnp.cos(ang), jnp.sin(ang)
    log_g = jnp.log1p(-jnp.exp2(-5.0 - jnp.arange(n_heads, dtype=F32)))
    ci = jnp.arange(chunk, dtype=F32)
    intra = jnp.exp(jnp.abs(ci[:, None] - ci[None, :])[None] * log_g[:, None, None])
    kdec = jnp.broadcast_to(jnp.exp((chunk - 1 - ci)[None, :] * log_g[:, None])[:, :, None],
                            (n_heads, chunk, dk))
    qdec = jnp.broadcast_to(jnp.exp((ci + 1.0)[None, :] * log_g[:, None])[:, :, None],
                            (n_heads, chunk, dk))
    gch = jnp.broadcast_to(jnp.exp(chunk * log_g)[:, None, None], (n_heads, 1, dv))

    def col(group):
        return pl.BlockSpec((None, tb, dk), lambda b, h, i: (b, i, group * n_heads + h))

    tab = pl.BlockSpec((tb, half), lambda b, h, i: (i, 0))
    per_head = lambda r, c: pl.BlockSpec((None, r, c), lambda b, h, i: (h, 0, 0))
    state = pl.BlockSpec((None, None, dk, dv), lambda b, h, i: (b, h, 0, 0))
    kern = functools.partial(_retention_kernel, chunk=chunk, n_sub=n_sub, q_scale=dk ** -0.5)
    return pl.pallas_call(
        kern,
        grid=(B, n_heads, T // tb),
        in_specs=[col(0), col(1), col(2), col(3), tab, tab,
                  per_head(chunk, chunk), per_head(chunk, dk), per_head(chunk, dk), per_head(1, dv),
                  pl.BlockSpec((1, dv), lambda b, h, i: (0, h)), state],
        out_specs=[pl.BlockSpec((None, tb, dv), lambda b, h, i: (b, i, h)), state],
        out_shape=[jax.ShapeDtypeStruct((B, T, n_heads * dv), BF16),
                   jax.ShapeDtypeStruct(s0.shape, F32)],
        scratch_shapes=[pltpu.VMEM((dk, dv), F32)],
        compiler_params=_cparams("parallel", "parallel", "arbitrary"),
        name="retention",
    )(proj, proj, proj, proj, cos, sin, intra, kdec, qdec, gch, gn_w, s0)


def _softplus(z):
    return jnp.maximum(z, 0.0) + jnp.log1p(jnp.exp(-jnp.abs(z)))


def _expm1(x):
    e = jnp.exp(x)
    em1 = e - 1.0
    return jnp.where(e == 1.0, x, jnp.where(em1 == -1.0, -1.0, em1 * x / jnp.log(e)))


def _gelu_tanh(x):
    return 0.5 * x * (1.0 + jnp.tanh(math.sqrt(2.0 / math.pi) * (x + 0.044715 * (x * x * x))))


def _rglru_kernel(x_ref, g_ref, cw_ref, cb_ref, wa_ref, ba_ref, wx_ref, bx_ref, lam_ref, h0_ref, c0_ref,
                  o_ref, hn_ref, cn_ref, xp_scr, a_scr, u_scr, h_scr, hc_scr, *, tb, conv_w, n_blocks):
    i = pl.program_id(1)
    hist = conv_w - 1
    lo = CONV_PAD - hist

    @pl.when(i == 0)
    def _():
        xp_scr[lo:CONV_PAD, :] = c0_ref[...]
        hc_scr[...] = h0_ref[...]

    @pl.when(i > 0)
    def _():
        xp_scr[lo:CONV_PAD, :] = xp_scr[tb + lo:tb + CONV_PAD, :]

    xp_scr[CONV_PAD:CONV_PAD + tb, :] = x_ref[...]
    xc = cb_ref[...] + xp_scr[lo:lo + tb, :] * cw_ref[0:1, :]
    for j in range(1, conv_w):
        xc = xc + xp_scr[lo + j:lo + j + tb, :] * cw_ref[j:j + 1, :]

    sp = RGLRU_C * _softplus(-lam_ref[...])
    bw = xc.shape[1] // n_blocks
    for n in range(n_blocks):
        cols = slice(n * bw, (n + 1) * bw)
        xb = xc[:, cols]
        xb16 = xb.astype(BF16)
        r = _sigmoid(_dot(xb16, wa_ref[n]) + ba_ref[:, cols])
        ig = _sigmoid(_dot(xb16, wx_ref[n]) + bx_ref[:, cols])
        log_a = -(r * sp[:, cols])
        a_scr[:, cols] = jnp.exp(log_a)
        u_scr[:, cols] = jnp.sqrt(-_expm1(2.0 * log_a)) * (ig * xb)

    def step(t, h):
        h = a_scr[pl.ds(t, 1), :] * h + u_scr[pl.ds(t, 1), :]
        h_scr[pl.ds(t, 1), :] = h
        return h

    h_last = lax.fori_loop(0, tb, step, hc_scr[...], unroll=8)
    hc_scr[...] = h_last
    o_ref[...] = (_gelu_tanh(g_ref[...]) * h_scr[...]).astype(o_ref.dtype)

    @pl.when(i == pl.num_programs(1) - 1)
    def _():
        hn_ref[...] = h_last
        cn_ref[...] = xp_scr[tb + lo:tb + CONV_PAD, :]


def rglru_mixer(proj, x_group, h0, conv0, conv_w, conv_b, w_a, b_a, w_x, b_x, lam):
    B, T, _ = proj.shape
    d_rnn = h0.shape[-1]
    cw = conv_w.shape[0]
    n_blocks = w_a.shape[0]
    tb = _pick(T, (256, 128, 64, 32, 16, 8))
    kern = functools.partial(_rglru_kernel, tb=tb, conv_w=cw, n_blocks=n_blocks)
    full = lambda *s: pl.BlockSpec(s, lambda b, i: (0,) * len(s))
    return pl.pallas_call(
        kern,
        grid=(B, T // tb),
        in_specs=[pl.BlockSpec((None, tb, d_rnn), lambda b, i: (b, i, x_group)),
                  pl.BlockSpec((None, tb, d_rnn), lambda b, i: (b, i, x_group + 1)),
                  full(cw, d_rnn), full(1, d_rnn),
                  full(*w_a.shape), full(1, d_rnn), full(*w_x.shape), full(1, d_rnn), full(1, d_rnn),
                  pl.BlockSpec((None, 1, d_rnn), lambda b, i: (b, 0, 0)),
                  pl.BlockSpec((None, cw - 1, d_rnn), lambda b, i: (b, 0, 0))],
        out_specs=[pl.BlockSpec((None, tb, d_rnn), lambda b, i: (b, i, 0)),
                   pl.BlockSpec((None, 1, d_rnn), lambda b, i: (b, 0, 0)),
                   pl.BlockSpec((None, cw - 1, d_rnn), lambda b, i: (b, 0, 0))],
        out_shape=[jax.ShapeDtypeStruct((B, T, d_rnn), BF16),
                   jax.ShapeDtypeStruct((B, 1, d_rnn), F32),
                   jax.ShapeDtypeStruct((B, cw - 1, d_rnn), F32)],
        scratch_shapes=[pltpu.VMEM((tb + CONV_PAD, d_rnn), F32), pltpu.VMEM((tb, d_rnn), F32),
                        pltpu.VMEM((tb, d_rnn), F32), pltpu.VMEM((tb, d_rnn), F32),
                        pltpu.VMEM((1, d_rnn), F32)],
        compiler_params=_cparams("parallel", "arbitrary"),
        name="rglru",
    )(proj, proj, conv_w, conv_b, w_a, b_a, w_x, b_x, lam, h0, conv0)


def _merge_kernel(pr_ref, pn_ref, wr_ref, wn_ref, g0_ref, g1_ref, o_ref):
    yr = _dot(pr_ref[...], wr_ref[...])
    yn = _dot(pn_ref[...], wn_ref[...])
    o_ref[...] = (_sigmoid(g0_ref[...]) * yr + _sigmoid(g1_ref[...]) * yn).astype(o_ref.dtype)


def merge_branches(pre_ret, pre_rnn, w_ret, w_rnn, proj, gate_off):
    M, kr = pre_ret.shape
    kn = pre_rnn.shape[1]
    D = w_ret.shape[1]
    tm = _pick(M, (1024, 512, 256, 128, 64, 32, 16, 8))
    tn = _pick(math.gcd(D, gate_off), (512, 256, 128))
    g0 = gate_off // tn
    g1 = (gate_off + D) // tn
    return pl.pallas_call(
        _merge_kernel,
        grid=(M // tm, D // tn),
        in_specs=[pl.BlockSpec((tm, kr), lambda i, j: (i, 0)),
                  pl.BlockSpec((tm, kn), lambda i, j: (i, 0)),
                  pl.BlockSpec((kr, tn), lambda i, j: (0, j)),
                  pl.BlockSpec((kn, tn), lambda i, j: (0, j)),
                  pl.BlockSpec((tm, tn), lambda i, j: (i, g0 + j)),
                  pl.BlockSpec((tm, tn), lambda i, j: (i, g1 + j))],
        out_specs=pl.BlockSpec((tm, tn), lambda i, j: (i, j)),
        out_shape=jax.ShapeDtypeStruct((M, D), BF16),
        compiler_params=_cparams("parallel", "parallel"),
        name="merge_branches",
    )(pre_ret, pre_rnn, w_ret, w_rnn, proj, proj)


def _pack_pair(x):
    half = x.shape[-1] // 2
    lo = lax.bitcast_convert_type(x[:, :half].astype(BF16).astype(F32), jnp.uint32) >> 16
    hi = lax.bitcast_convert_type(x[:, half:].astype(BF16).astype(F32), jnp.uint32) & jnp.uint32(0xFFFF0000)
    return lo | hi


def _unpack_pair(w):
    lo = lax.bitcast_convert_type(w << 16, F32)
    hi = lax.bitcast_convert_type(w & jnp.uint32(0xFFFF0000), F32)
    return lo, hi


def _post1_kernel(x_ref, mix_ref, g1_ref, sc_ref, sh_ref, w_ref, b_ref, x1_ref, h2_ref, *, alpha):
    x1 = _ln(alpha * x_ref[...] + g1_ref[...] * mix_ref[...]) * w_ref[...] + b_ref[...]
    x1_ref[...] = x1
    h2_ref[...] = _pack_pair(_ln(x1) * (1.0 + sc_ref[...]) + sh_ref[...])


def post_mixer(x, mix, g1, sc2, sh2, ln_w, ln_b, alpha):
    G, R, D = x.shape
    tm = _pick(R, (256, 128, 64, 32, 16, 8))
    row = pl.BlockSpec((None, tm, D), lambda g, i: (g, i, 0))
    vec = pl.BlockSpec((1, D), lambda g, i: (0, 0))
    return pl.pallas_call(
        functools.partial(_post1_kernel, alpha=alpha),
        grid=(G, R // tm),
        in_specs=[row, row, _mod_spec(g1, tm), _mod_spec(sc2, tm), _mod_spec(sh2, tm), vec, vec],
        out_specs=[row, pl.BlockSpec((None, tm, D // 2), lambda g, i: (g, i, 0))],
        out_shape=[jax.ShapeDtypeStruct((G, R, D), F32), jax.ShapeDtypeStruct((G, R, D // 2), jnp.uint32)],
        compiler_params=_cparams("parallel", "parallel"),
        name="post_mixer",
    )(x, mix, g1, sc2, sh2, ln_w, ln_b)


def _router_kernel(h_ref, wlo_ref, whi_ref, b_ref, idx_ref, wt_ref, rank_ref, cnt_ref, carry_scr, *, n_experts):
    tm = h_ref.shape[0]
    gsz = n_experts // N_GROUPS
    neg = -jnp.inf

    @pl.when(pl.program_id(0) == 0)
    def _():
        carry_scr[...] = jnp.zeros_like(carry_scr)

    h_lo, h_hi = _unpack_pair(h_ref[...])
    s = _sigmoid(_dot(h_lo.astype(BF16), wlo_ref[...]) + _dot(h_hi.astype(BF16), whi_ref[...]))
    sb = s + b_ref[...]
    lane = lax.broadcasted_iota(jnp.int32, (tm, n_experts), 1).astype(F32)
    grp = jnp.zeros((tm, n_experts), F32)
    for g in range(1, N_GROUPS):
        grp = grp + (lane >= float(g * gsz)).astype(F32)
    lane_o = lax.broadcasted_iota(jnp.int32, (tm, LANES), 1).astype(F32)

    def argmax_first(v, ids, big):
        m = jnp.max(v, axis=-1, keepdims=True)
        first = jnp.min(jnp.where(v == m, ids, float(big)), axis=-1, keepdims=True)
        return m, first

    gscore = jnp.full((tm, LANES), neg, F32)
    for g in range(N_GROUPS):
        vg = jnp.where(grp == g, sb, neg)
        m1, i1 = argmax_first(vg, lane, n_experts)
        m2 = jnp.max(jnp.where(lane == i1, neg, vg), axis=-1, keepdims=True)
        gscore = jnp.where(lane_o == g, m1 + m2, gscore)
    keep = jnp.zeros((tm, n_experts), jnp.bool_)
    for _ in range(TOPK_GROUPS):
        _, gi = argmax_first(gscore, lane_o, LANES)
        gscore = jnp.where(lane_o == gi, neg, gscore)
        keep = jnp.logical_or(keep, grp == gi)
    sel = jnp.where(keep, sb, neg)
    idx_o = jnp.zeros((tm, LANES), F32)
    wt_o = jnp.zeros((tm, LANES), F32)
    chosen = []
    taken = jnp.zeros((tm, n_experts), F32)
    for k in range(TOP_K):
        _, ei = argmax_first(sel, lane, n_experts)
        hit = lane == ei
        sel = jnp.where(hit, neg, sel)
        wk = jnp.sum(jnp.where(hit, s, 0.0), axis=-1, keepdims=True)
        idx_o = jnp.where(lane_o == k, ei, idx_o)
        wt_o = jnp.where(lane_o == k, wk, wt_o)
        taken = jnp.where(hit, 1.0, taken)
        chosen.append(ei)
    wt_o = wt_o / jnp.sum(wt_o, axis=-1, keepdims=True) * ROUTED_SCALE
    idx_ref[...] = idx_o.astype(jnp.int32)
    wt_ref[...] = wt_o
    ri = lax.broadcasted_iota(jnp.int32, (tm, tm), 0)
    ci = lax.broadcasted_iota(jnp.int32, (tm, tm), 1)
    before = _dot((ri > ci).astype(BF16), taken.astype(BF16)) + carry_scr[...]
    rank_o = jnp.zeros((tm, LANES), F32)
    for k in range(TOP_K):
        rk = jnp.sum(jnp.where(lane == chosen[k], before, 0.0), axis=-1, keepdims=True)
        rank_o = jnp.where(lane_o == k, rk, rank_o)
    rank_ref[...] = rank_o.astype(jnp.int32)
    carry_scr[...] = carry_scr[...] + jnp.sum(taken, axis=0, keepdims=True)
    cnt_ref[...] = carry_scr[...]


def router(h2p, w_router, bias):
    M, H = h2p.shape
    E = w_router.shape[1]
    tm = _pick(M, (256, 128, 64, 32, 16, 8))
    out = pl.BlockSpec((tm, LANES), lambda i: (i, 0))
    return pl.pallas_call(
        functools.partial(_router_kernel, n_experts=E),
        grid=(M // tm,),
        in_specs=[pl.BlockSpec((tm, H), lambda i: (i, 0)),
                  pl.BlockSpec((H, E), lambda i: (0, 0)),
                  pl.BlockSpec((H, E), lambda i: (1, 0)),
                  pl.BlockSpec((1, E), lambda i: (0, 0))],
        out_specs=[out, out, out, pl.BlockSpec((1, E), lambda i: (0, 0))],
        out_shape=[jax.ShapeDtypeStruct((M, LANES), jnp.int32), jax.ShapeDtypeStruct((M, LANES), F32),
                   jax.ShapeDtypeStruct((M, LANES), jnp.int32), jax.ShapeDtypeStruct((1, E), F32)],
        scratch_shapes=[pltpu.VMEM((1, E), F32)],
        compiler_params=_cparams("arbitrary"),
        name="router",
    )(h2p, w_router, w_router, bias)


def _row_copy(src, src_row, dst, dst_row, sem):
    return pltpu.make_async_copy(src.at[pl.ds(src_row, 1)], dst.at[pl.ds(dst_row, 1)], sem)


def _pos_copy(pos_hbm, pos_smem, isem, tile, s, n_rows):
    start = pl.multiple_of(s * n_rows, n_rows)
    return pltpu.make_async_copy(pos_hbm.at[tile], pos_smem.at[pl.ds(start, n_rows)], isem.at[s])


def _pad_chunks(rows):
    n, out = rows // 2, []
    while n >= SUBLANES:
        out.append(n)
        n //= 2
    return out


def _dispatch_kernel(pstart_ref, count_ref, padded_ref, nused_ref, pos_hbm, h_ref, xs_hbm,
                     pos_smem, xbuf, zbuf, isem, dsem, zsem, *, tm, n_experts, bm):
    i = pl.program_id(0)
    nt = pl.num_programs(0)
    slot = i % 2
    nxt = 1 - slot
    n_rows = tm * TOP_K

    def pad_fill(wait):
        def body(e, c):
            npad = padded_ref[e] - count_ref[e]
            off = pstart_ref[e] + count_ref[e]
            n_single = (-count_ref[e]) & (SUBLANES - 1)
            for j in range(SUBLANES - 1):
                cp = _row_copy(zbuf, 0, xs_hbm, off + j, zsem)

                @pl.when(j < n_single)
                def _():
                    cp.wait() if wait else cp.start()

            off = pl.multiple_of(off + n_single, SUBLANES)
            n_tiled = npad - n_single
            for n in _pad_chunks(bm):
                cp = pltpu.make_async_copy(zbuf.at[pl.ds(0, n)], xs_hbm.at[pl.ds(off, n)], zsem)

                @pl.when((n_tiled & n) != 0)
                def _():
                    cp.wait() if wait else cp.start()

                off = pl.multiple_of(off + (n_tiled & n), SUBLANES)
            return c
        lax.fori_loop(0, n_experts, body, 0)

        def tail(b, c):
            for part in range(2):
                row0 = pl.multiple_of(b * bm + part * (bm // 2), SUBLANES)
                cp = pltpu.make_async_copy(zbuf, xs_hbm.at[pl.ds(row0, bm // 2)], zsem)
                cp.wait() if wait else cp.start()
            return c
        lax.fori_loop(nused_ref[0], xs_hbm.shape[0] // bm, tail, 0)

    @pl.when(i == 0)
    def _():
        zbuf[...] = jnp.zeros_like(zbuf)
        pad_fill(False)
        _pos_copy(pos_hbm, pos_smem, isem, 0, 0, n_rows).start()
        _pos_copy(pos_hbm, pos_smem, isem, 0, 0, n_rows).wait()
        pad_fill(True)

    @pl.when(i + 1 < nt)
    def _():
        _pos_copy(pos_hbm, pos_smem, isem, i + 1, nxt, n_rows).start()

    xbuf[slot] = h_ref[...]
    base = slot * n_rows

    for r in range(tm):
        for k in range(TOP_K):
            _row_copy(xbuf.at[slot], r, xs_hbm, pos_smem[base + r * TOP_K + k], dsem.at[slot]).start()

    def wait_all(s):
        def body(r, c):
            _row_copy(xbuf.at[s], 0, xs_hbm, 0, dsem.at[s]).wait()
            return c
        lax.fori_loop(0, n_rows, body, 0, unroll=8)

    @pl.when(i > 0)
    def _():
        wait_all(nxt)

    @pl.when(i + 1 < nt)
    def _():
        _pos_copy(pos_hbm, pos_smem, isem, i + 1, nxt, n_rows).wait()

    @pl.when(i == nt - 1)
    def _():
        wait_all(slot)


def dispatch_tokens(h2p, pos_tiles, pstart, counts, padded, n_used, n_slots, bm):
    M, H = h2p.shape
    nt, n_rows = pos_tiles.shape
    tm = n_rows // TOP_K
    any_spec = pl.BlockSpec(memory_space=pl.ANY)
    grid_spec = pltpu.PrefetchScalarGridSpec(
        num_scalar_prefetch=4,
        grid=(nt,),
        in_specs=[any_spec, pl.BlockSpec((tm, H), lambda i, *_: (i, 0))],
        out_specs=any_spec,
        scratch_shapes=[pltpu.SMEM((2 * n_rows,), jnp.int32), pltpu.VMEM((2, tm, H), jnp.uint32),
                        pltpu.VMEM((bm // 2, H), jnp.uint32),
                        pltpu.SemaphoreType.DMA((2,)), pltpu.SemaphoreType.DMA((2,)),
                        pltpu.SemaphoreType.DMA],
    )
    return pl.pallas_call(
        functools.partial(_dispatch_kernel, tm=tm, n_experts=pstart.shape[0], bm=bm),
        grid_spec=grid_spec,
        out_shape=jax.ShapeDtypeStruct((n_slots, H), jnp.uint32),
        compiler_params=_cparams("arbitrary"),
        name="dispatch",
    )(pstart, counts, padded, n_used, pos_tiles, h2p)


def _experts_kernel(blk_e_ref, first_ref, next_ref, nused_ref, xs_ref, w1_hbm, w3_hbm, w2_hbm, y_ref,
                    st1, st3, st2, wb1, wb3, wb2, wsem):
    i = pl.program_id(0)
    half = xs_ref.shape[1]

    mats = ((w1_hbm, st1, wb1), (w3_hbm, st3, wb3), (w2_hbm, st2, wb2))

    def weight_copy(m, e):
        return pltpu.make_async_copy(mats[m][0].at[e], mats[m][1], wsem.at[m])

    @pl.when(i == 0)
    def _():
        for m in range(3):
            weight_copy(m, blk_e_ref[0]).start()

    @pl.when(first_ref[i] == 1)
    def _():
        for m in range(3):
            weight_copy(m, blk_e_ref[i]).wait()
        nxt = jnp.maximum(next_ref[i], 0)
        for m, (_, st, wb) in enumerate(mats):
            wb[...] = st[...].astype(BF16)

            @pl.when(next_ref[i] >= 0)
            def _():
                weight_copy(m, nxt).start()

    @pl.when(i < nused_ref[0])
    def _():
        x_lo, x_hi = _unpack_pair(xs_ref[...])
        x_lo = x_lo.astype(BF16)
        x_hi = x_hi.astype(BF16)
        h1 = _dot(x_lo, wb1[0:half, :]) + _dot(x_hi, wb1[half:2 * half, :])
        h3 = _dot(x_lo, wb3[0:half, :]) + _dot(x_hi, wb3[half:2 * half, :])
        hh = (h1 * _sigmoid(h1) * h3).astype(BF16)
        y_ref[...] = _pack_pair(_dot(hh, wb2[...]))

    @pl.when(i >= nused_ref[0])
    def _():
        y_ref[...] = jnp.zeros_like(y_ref)


def routed_expert_blocks(xs, blk_e, first, next_e, n_used, w1, w3, w2, bm):
    n_slots, H = xs.shape
    nb = n_slots // bm
    _, D, F = w1.shape
    any_spec = pl.BlockSpec(memory_space=pl.ANY)
    grid_spec = pltpu.PrefetchScalarGridSpec(
        num_scalar_prefetch=4,
        grid=(nb,),
        in_specs=[pl.BlockSpec((bm, H), lambda i, be, fi, ne, nu: (jnp.minimum(i, nu[0] - 1), 0)),
                  any_spec, any_spec, any_spec],
        out_specs=pl.BlockSpec((bm, H), lambda i, be, fi, ne, nu: (i, 0)),
        scratch_shapes=[pltpu.VMEM((D, F), F32), pltpu.VMEM((D, F), F32), pltpu.VMEM((F, D), F32),
                        pltpu.VMEM((D, F), BF16), pltpu.VMEM((D, F), BF16), pltpu.VMEM((F, D), BF16),
                        pltpu.SemaphoreType.DMA((3,))],
    )
    return pl.pallas_call(
        _experts_kernel,
        grid_spec=grid_spec,
        out_shape=jax.ShapeDtypeStruct((n_slots, H), jnp.uint32),
        compiler_params=_cparams("arbitrary"),
        name="routed_experts",
    )(blk_e, first, next_e, n_used, xs, w1, w3, w2)


def _final_kernel(pos_hbm, y_hbm, x1_ref, h2_ref, wt_ref, g2_ref, ws1_ref, ws3_ref, ws2_ref,
                  w_ref, b_ref, o_ref, ybuf, pos_smem, gsem, isem, *, tm, alpha, tile_off):
    g = pl.program_id(0)
    i = pl.program_id(1)
    nt = pl.num_programs(1)
    step = g * nt + i
    nsteps = pl.num_programs(0) * nt
    slot = step % 2
    nxt = 1 - slot
    n_rows = tm * TOP_K
    half = h2_ref.shape[1]

    def pos_copy(st, s):
        return _pos_copy(pos_hbm, pos_smem, isem, tile_off + st, s, n_rows)

    def issue_rows(s):
        base = s * n_rows
        for r in range(tm):
            for k in range(TOP_K):
                _row_copy(y_hbm, pos_smem[base + r * TOP_K + k], ybuf.at[s], k * tm + r, gsem.at[s]).start()

    def wait_rows(s):
        def body(r, c):
            _row_copy(y_hbm, 0, ybuf.at[s], r, gsem.at[s]).wait()
            return c
        lax.fori_loop(0, n_rows, body, 0, unroll=8)

    @pl.when(step == 0)
    def _():
        pos_copy(0, 0).start()
        pos_copy(0, 0).wait()
        issue_rows(0)

        @pl.when(nsteps > 1)
        def _():
            pos_copy(1, 1).start()

    @pl.when(step + 1 < nsteps)
    def _():
        pos_copy(step + 1, nxt).wait()

    issue_rows(nxt)

    h_lo, h_hi = _unpack_pair(h2_ref[...])
    h_lo = h_lo.astype(BF16)
    h_hi = h_hi.astype(BF16)
    a1 = _dot(h_lo, ws1_ref[0:half, :]) + _dot(h_hi, ws1_ref[half:2 * half, :])
    a3 = _dot(h_lo, ws3_ref[0:half, :]) + _dot(h_hi, ws3_ref[half:2 * half, :])
    shared = _dot((a1 * _sigmoid(a1) * a3).astype(BF16), ws2_ref[...])
    ffn_lo = shared[:, :half]
    ffn_hi = shared[:, half:]

    wait_rows(slot)

    @pl.when(step + 2 < nsteps)
    def _():
        pos_copy(step + 2, slot).start()

    wt = wt_ref[...]
    for k in range(TOP_K):
        y_lo, y_hi = _unpack_pair(ybuf[slot, k * tm:(k + 1) * tm, :])
        wk = wt[:, k:k + 1]
        ffn_lo = ffn_lo + wk * y_lo
        ffn_hi = ffn_hi + wk * y_hi
    ffn = jnp.concatenate([ffn_lo, ffn_hi], axis=-1)
    o_ref[...] = _ln(alpha * x1_ref[...] + g2_ref[...] * ffn) * w_ref[...] + b_ref[...]

    @pl.when(step == nsteps - 1)
    def _():
        wait_rows(nxt)


def combine_and_finish(x1, h2p, wts, pos_tiles, tile_off, y, g2, ws1, ws3, ws2, ln_w, ln_b, alpha):
    G, R, D = x1.shape
    H = D // 2
    n_rows = pos_tiles.shape[1]
    tm = n_rows // TOP_K
    F = ws1.shape[1]
    assert G * (R // tm) >= 2, "the row prefetch needs at least two token tiles"
    any_spec = pl.BlockSpec(memory_space=pl.ANY)
    row = lambda w: pl.BlockSpec((None, tm, w), lambda g, i: (g, i, 0))
    const = lambda r, c: pl.BlockSpec((r, c), lambda g, i: (0, 0))
    kern = functools.partial(_final_kernel, tm=tm, alpha=alpha, tile_off=tile_off)
    return pl.pallas_call(
        kern,
        grid=(G, R // tm),
        in_specs=[any_spec, any_spec, row(D), row(H), row(LANES), _mod_spec(g2, tm),
                  const(D, F), const(D, F), const(F, D), const(1, D), const(1, D)],
        out_specs=row(D),
        out_shape=jax.ShapeDtypeStruct((G, R, D), F32),
        scratch_shapes=[pltpu.VMEM((2, n_rows, H), jnp.uint32), pltpu.SMEM((2 * n_rows,), jnp.int32),
                        pltpu.SemaphoreType.DMA((2,)), pltpu.SemaphoreType.DMA((2,))],
        compiler_params=_cparams("arbitrary", "arbitrary"),
        name="combine_finish",
    )(pos_tiles, y, x1, h2p, wts, g2, ws1, ws3, ws2, ln_w, ln_b)


def _slots_kernel(idx_ref, rank_ref, pstart_ref, pos_ref):
    tm = idx_ref.shape[0]
    E = pstart_ref.shape[1]
    lane = lax.broadcasted_iota(jnp.int32, (tm, E), 1)
    lane_o = lax.broadcasted_iota(jnp.int32, (tm, LANES), 1)
    idx = idx_ref[...]
    ps = pstart_ref[...]
    start = jnp.zeros((tm, LANES), F32)
    for k in range(TOP_K):
        sk = jnp.sum(jnp.where(lane == idx[:, k:k + 1], ps, 0.0), axis=-1, keepdims=True)
        start = jnp.where(lane_o == k, sk, start)
    pos_ref[...] = start.astype(jnp.int32) + rank_ref[...]


def assignment_slots(idx, rank, pstart):
    M = idx.shape[0]
    E = pstart.shape[1]
    tm = _pick(M, (256, 128, 64, 32, 16, 8))
    blk = pl.BlockSpec((tm, LANES), lambda i: (i, 0))
    return pl.pallas_call(
        _slots_kernel,
        grid=(M // tm,),
        in_specs=[blk, blk, pl.BlockSpec((1, E), lambda i: (0, 0))],
        out_specs=blk,
        out_shape=jax.ShapeDtypeStruct((M, LANES), jnp.int32),
        compiler_params=_cparams("parallel"),
        name="assignment_slots",
    )(idx, rank, pstart)


def _block_tables(counts, n_blocks, bm):
    E = counts.shape[0]
    padded = (counts + bm - 1) // bm * bm
    pend = jnp.cumsum(padded)
    pstart = pend - padded
    n_used = pend[-1] // bm
    blk = jnp.arange(n_blocks, dtype=jnp.int32)
    blk_e = jnp.minimum(jnp.sum((pend[None, :] <= (blk * bm)[:, None]).astype(jnp.int32), axis=1), E - 1)
    prev_e = jnp.concatenate([jnp.full((1,), -1, jnp.int32), blk_e[:-1]])
    first = ((blk_e != prev_e) & (blk < n_used)).astype(jnp.int32)
    ids = jnp.arange(E, dtype=jnp.int32)
    active = jnp.where(counts > 0, ids, E)
    later = lax.cummin(active[::-1])[::-1]
    nxt = jnp.concatenate([later[1:], jnp.full((1,), E, jnp.int32)])
    nxt = jnp.where(nxt >= E, -1, nxt)
    next_e = jnp.sum(jnp.where(ids[None, :] == blk_e[:, None], nxt[None, :], 0), axis=1).astype(jnp.int32)
    i32 = lambda v: v.astype(jnp.int32)
    return i32(pstart), i32(padded), blk_e, first, next_e, i32(n_used).reshape(1)


def _assignment_tiles(v, tm):
    M = v.shape[0]
    return v[:, :TOP_K].reshape(M // tm, tm * TOP_K)


def _stream_mixers(x, mods, pos, s_ret, h_rnn, conv_buf, lw, alpha):
    B, T, D = x.shape
    n_heads, d_ret = lw["n_heads"], lw["d_ret"]
    d_rnn = lw["d_rnn"]
    per_row = mods["per_row"]
    xg = x.reshape(1, B * T, D) if per_row else x
    h = ln_mod(xg, mods["sc1"], mods["sh1"]).reshape(B * T, D)
    proj = matmul(h, lw["w_in"]).reshape(B, T, -1)
    pre_ret, s_new = retention_mixer(proj, pos, s_ret, lw["ret_gn_w"], n_heads, d_ret)
    x_group = (4 * d_ret) // d_rnn
    pre_rnn, h_new, conv_new = rglru_mixer(proj, x_group, h_rnn[:, None, :], conv_buf, lw["conv_w"],
                                           lw["conv_b"], lw["w_a"], lw["b_a"], lw["w_x"], lw["b_x"],
                                           lw["rg_lambda"])
    mixin = merge_branches(pre_ret.reshape(B * T, d_ret), pre_rnn.reshape(B * T, d_rnn),
                           lw["w_br_ret"], lw["w_br_rnn"], proj.reshape(B * T, -1), 4 * d_ret + 2 * d_rnn)
    mix = matmul(mixin, lw["w_o"])
    x1, h2 = post_mixer(xg, mix.reshape(xg.shape), mods["g1"], mods["sc2"], mods["sh2"],
                        lw["ln1_w"], lw["ln1_b"], alpha)
    return x1, h2, s_new, h_new[:, 0, :], conv_new


def _split_mods(mod, B, T, per_row):
    D = mod.shape[1] // 6
    parts = mod.reshape(B, 6, D)
    out = {"per_row": per_row}
    for n, name in enumerate(("sh1", "sc1", "g1", "sh2", "sc2", "g2")):
        p = parts[:, n, :]
        out[name] = jnp.repeat(p, T, axis=0)[None] if per_row else p[:, None, :]
    return out


def kernel(x_prompt, x_sample, state_ret, state_rglru, state_conv, c_prompt, c_sample, w_ada, b_ada, w_in, ret_gn_w, conv_w, conv_b, w_a, b_a, w_x, b_x, rg_lambda, w_br_ret, w_br_rnn, w_o, ln1_w, ln1_b, w_router, router_bias, w1, w3, w2, ws1, ws3, ws2, ln2_w, ln2_b):
    B, T, D = x_prompt.shape
    Bs, Ts, _ = x_sample.shape
    depth = w_ada.shape[0]
    n_heads, dk, dv = state_ret.shape[2:]
    d_ret = n_heads * dk
    d_rnn = state_rglru.shape[-1]
    cw = conv_w.shape[1]
    n_experts = w_router.shape[-1]
    alpha = (2.0 * depth) ** 0.25
    pos_p = jnp.arange(T, dtype=jnp.int32)
    pos_s = PAST_LEN + jnp.arange(Ts, dtype=jnp.int32)
    row = lambda v: v.reshape(1, -1)

    yp, ys = x_prompt, x_sample
    outs = [[] for _ in range(6)]
    for l in range(depth):
        lw = dict(n_heads=n_heads, d_ret=d_ret, d_rnn=d_rnn,
                  w_in=w_in[l].astype(BF16), ret_gn_w=row(ret_gn_w[l]), conv_w=conv_w[l], conv_b=row(conv_b[l]),
                  w_a=w_a[l].astype(BF16), b_a=row(b_a[l]), w_x=w_x[l].astype(BF16), b_x=row(b_x[l]),
                  rg_lambda=row(rg_lambda[l]), w_br_ret=w_br_ret[l].astype(BF16),
                  w_br_rnn=w_br_rnn[l].astype(BF16), w_o=w_o[l].astype(BF16),
                  ln1_w=row(ln1_w[l]), ln1_b=row(ln1_b[l]))
        mod = ada_mod(jnp.concatenate([c_prompt, c_sample], axis=0), w_ada[l], row(b_ada[l]))
        mods_p = _split_mods(mod[:B], B, T, per_row=False)
        mods_s = _split_mods(mod[B:], Bs, Ts, per_row=True)

        zero_ret = jnp.zeros((B, n_heads, dk, dv), F32)
        zero_h = jnp.zeros((B, d_rnn), F32)
        zero_conv = jnp.zeros((B, cw - 1, d_rnn), F32)
        x1p, h2p, r1, h1, c1 = _stream_mixers(yp, mods_p, pos_p, zero_ret, zero_h, zero_conv, lw, alpha)
        x1s, h2s, r2, hh2, c2 = _stream_mixers(ys, mods_s, pos_s, state_ret[l], state_rglru[l],
                                               state_conv[l], lw, alpha)

        Mp, Ms = B * T, Bs * Ts
        M = Mp + Ms
        bm = EXPERT_ROWS
        tm = math.gcd(math.gcd(T, Ms // 2), TOKEN_TILE)
        h2_all = jnp.concatenate([h2p.reshape(Mp, D // 2), h2s.reshape(Ms, D // 2)], axis=0)
        idx_l, wt_l, rank_l, cnt = router(h2_all, w_router[l].astype(BF16), row(router_bias[l]))
        n_blocks = -(-(M * TOP_K) // bm) + n_experts
        pstart, padded, blk_e, first, next_e, n_used = _block_tables(cnt[0].astype(jnp.int32), n_blocks, bm)
        pos_t = _assignment_tiles(assignment_slots(idx_l, rank_l, row(pstart).astype(F32)), tm)
        xs = dispatch_tokens(h2_all, pos_t, pstart, cnt[0].astype(jnp.int32), padded, n_used, n_blocks * bm, bm)
        y_slots = routed_expert_blocks(xs, blk_e, first, next_e, n_used, w1[l], w3[l], w2[l], bm)
        ws = (ws1[l].astype(BF16), ws3[l].astype(BF16), ws2[l].astype(BF16))
        yp = combine_and_finish(x1p, h2p, wt_l[:Mp].reshape(B, T, LANES), pos_t, 0,
                                y_slots, mods_p["g2"], *ws, row(ln2_w[l]), row(ln2_b[l]), alpha)
        ys = combine_and_finish(x1s, h2s, wt_l[Mp:].reshape(1, Ms, LANES), pos_t, Mp // tm,
                                y_slots, mods_s["g2"], *ws, row(ln2_w[l]), row(ln2_b[l]), alpha)
        ys = ys.reshape(Bs, Ts, D)
        for lst, val in zip(outs, (r1, h1, c1, r2, hh2, c2)):
            lst.append(val)
    return (yp, ys) + tuple(jnp.stack(o) for o in outs)
```

```python
import functools
import math

import jax
import jax.numpy as jnp
from jax import lax
from jax.experimental import pallas as pl
from jax.experimental.pallas import tpu as pltpu

CHUNK = 64
TOP_K = 8
N_GROUPS = 8
TOPK_GROUPS = 4
ROUTED_SCALE = 2.5
RGLRU_C = 8.0
PAST_LEN = 4096
ROPE_BASE = 10000.0
LN_EPS = 1e-5

F32 = jnp.float32
BF16 = jnp.bfloat16

V7X_VMEM_LIMIT_BYTES = 56 * 1024 * 1024
LANES = 128
SUBLANES = 8
EXPERT_ROWS = 256
TOKEN_TILE = 128
CONV_PAD = 8


def _cparams(*sem):
    return pltpu.CompilerParams(dimension_semantics=sem, vmem_limit_bytes=V7X_VMEM_LIMIT_BYTES)


def _pick(n, cands):
    for c in cands:
        if n % c == 0:
            return c
    return n


def _ln(x):
    mu = jnp.mean(x, axis=-1, keepdims=True)
    xc = x - mu
    var = jnp.mean(xc * xc, axis=-1, keepdims=True)
    return xc * lax.rsqrt(var + LN_EPS)


def _sigmoid(x):
    return jax.nn.sigmoid(x)


def _dot(a, b):
    return jnp.dot(a, b, preferred_element_type=F32)


def _ada_kernel(c_ref, w_ref, b_ref, o_ref):
    c = c_ref[...]
    s = (c * _sigmoid(c)).astype(BF16)
    o_ref[...] = _dot(s, w_ref[...].astype(BF16)) + b_ref[...]


def ada_mod(c, w, b):
    R, D = c.shape
    N = w.shape[1]
    tn = _pick(N, (512, 256, 128))
    return pl.pallas_call(
        _ada_kernel,
        grid=(N // tn,),
        in_specs=[pl.BlockSpec((R, D), lambda j: (0, 0)),
                  pl.BlockSpec((D, tn), lambda j: (0, j)),
                  pl.BlockSpec((1, tn), lambda j: (0, j))],
        out_specs=pl.BlockSpec((R, tn), lambda j: (0, j)),
        out_shape=jax.ShapeDtypeStruct((R, N), F32),
        compiler_params=_cparams("arbitrary"),
        name="ada_mod",
    )(c, w, b)


def _mod_spec(arr, tm):
    D = arr.shape[-1]
    if arr.shape[1] == 1:
        return pl.BlockSpec((None, 1, D), lambda g, i: (g, 0, 0))
    return pl.BlockSpec((None, tm, D), lambda g, i: (g, i, 0))


def _lnmod_kernel(x_ref, sc_ref, sh_ref, o_ref):
    y = _ln(x_ref[...]) * (1.0 + sc_ref[...]) + sh_ref[...]
    o_ref[...] = y.astype(o_ref.dtype)


def ln_mod(x, sc, sh):
    G, R, D = x.shape
    tm = _pick(R, (512, 256, 128, 64, 32, 16, 8))
    row = pl.BlockSpec((None, tm, D), lambda g, i: (g, i, 0))
    return pl.pallas_call(
        _lnmod_kernel,
        grid=(G, R // tm),
        in_specs=[row, _mod_spec(sc, tm), _mod_spec(sh, tm)],
        out_specs=row,
        out_shape=jax.ShapeDtypeStruct((G, R, D), BF16),
        compiler_params=_cparams("parallel", "parallel"),
        name="ln_mod",
    )(x, sc, sh)


def _mm_kernel(a_ref, w_ref, o_ref):
    o_ref[...] = _dot(a_ref[...], w_ref[...]).astype(o_ref.dtype)


def matmul(a, w, out_dtype=F32):
    M, K = a.shape
    N = w.shape[1]
    tm = _pick(M, (1024, 512, 256, 128, 64, 32, 16, 8))
    tn = _pick(N, (1024, 512, 256, 128))
    return pl.pallas_call(
        _mm_kernel,
        grid=(M // tm, N // tn),
        in_specs=[pl.BlockSpec((tm, K), lambda i, j: (i, 0)),
                  pl.BlockSpec((K, tn), lambda i, j: (0, j))],
        out_specs=pl.BlockSpec((tm, tn), lambda i, j: (i, j)),
        out_shape=jax.ShapeDtypeStruct((M, N), out_dtype),
        compiler_params=_cparams("parallel", "parallel"),
        name="matmul",
    )(a, w)


def _rot(x, cos, sin):
    half = x.shape[-1] // 2
    x1 = x[:, :half]
    x2 = x[:, half:]
    return jnp.concatenate([x1 * cos - x2 * sin, x1 * sin + x2 * cos], axis=-1)


def _retention_kernel(q_ref, k_ref, v_ref, g_ref, cos_ref, sin_ref, intra_ref, kdec_ref, qdec_ref,
                      gch_ref, gnw_ref, s0_ref, o_ref, sfin_ref, s_scr, *, chunk, n_sub, q_scale):
    i = pl.program_id(2)

    @pl.when(i == 0)
    def _():
        s_scr[...] = s0_ref[...]

    intra = intra_ref[...]
    kdec = kdec_ref[...]
    qdec = qdec_ref[...]
    gch = gch_ref[...]
    gnw = gnw_ref[...]
    for c in range(n_sub):
        rows = pl.ds(c * chunk, chunk)
        cos = cos_ref[rows, :]
        sin = sin_ref[rows, :]
        q = _rot(q_ref[rows, :], cos, sin) * q_scale
        k = _rot(k_ref[rows, :], cos, sin)
        v = v_ref[rows, :].astype(BF16)
        s_prev = s_scr[...]
        scores = lax.dot_general(q.astype(BF16), k.astype(BF16), (((1,), (1,)), ((), ())),
                                 preferred_element_type=F32) * intra
        o = _dot(scores.astype(BF16), v) + _dot((q * qdec).astype(BF16), s_prev.astype(BF16))
        kv = lax.dot_general((k * kdec).astype(BF16), v, (((0,), (0,)), ((), ())),
                             preferred_element_type=F32)
        s_scr[...] = gch * s_prev + kv
        on = _ln(o) * gnw
        g = g_ref[rows, :]
        o_ref[rows, :] = (g * _sigmoid(g) * on).astype(o_ref.dtype)

    @pl.when(i == pl.num_programs(2) - 1)
    def _():
        sfin_ref[...] = s_scr[...]


def retention_mixer(proj, pos, s0, gn_w, n_heads, d_ret):
    B, T, _ = proj.shape
    dk = d_ret // n_heads
    dv = s0.shape[-1]
    chunk = min(T, CHUNK)
    n_chunks = T // chunk
    n_sub = _pick(n_chunks, (8, 4, 2, 1))
    tb = n_sub * chunk
    half = dk // 2
    inv = ROPE_BASE ** (-jnp.arange(half, dtype=F32) / half)
    ang = pos.astype(F32)[:, None] * inv[None, :]
    cos, sin = jnp.cos(ang), jnp.sin(ang)
    log_g = jnp.log1p(-jnp.exp2(-5.0 - jnp.arange(n_heads, dtype=F32)))
    ci = jnp.arange(chunk, dtype=F32)
    intra = jnp.exp(jnp.abs(ci[:, None] - ci[None, :])[None] * log_g[:, None, None])
    kdec = jnp.broadcast_to(jnp.exp((chunk - 1 - ci)[None, :] * log_g[:, None])[:, :, None],
                            (n_heads, chunk, dk))
    qdec = jnp.broadcast_to(jnp.exp((ci + 1.0)[None, :] * log_g[:, None])[:, :, None],
                            (n_heads, chunk, dk))
    gch = jnp.broadcast_to(jnp.exp(chunk * log_g)[:, None, None], (n_heads, 1, dv))

    def col(group):
        return pl.BlockSpec((None, tb, dk), lambda b, h, i: (b, i, group * n_heads + h))

    tab = pl.BlockSpec((tb, half), lambda b, h, i: (i, 0))
    per_head = lambda r, c: pl.BlockSpec((None, r, c), lambda b, h, i: (h, 0, 0))
    state = pl.BlockSpec((None, None, dk, dv), lambda b, h, i: (b, h, 0, 0))
    kern = functools.partial(_retention_kernel, chunk=chunk, n_sub=n_sub, q_scale=dk ** -0.5)
    return pl.pallas_call(
        kern,
        grid=(B, n_heads, T // tb),
        in_specs=[col(0), col(1), col(2), col(3), tab, tab,
                  per_head(chunk, chunk), per_head(chunk, dk), per_head(chunk, dk), per_head(1, dv),
                  pl.BlockSpec((1, dv), lambda b, h, i: (0, h)), state],
        out_specs=[pl.BlockSpec((None, tb, dv), lambda b, h, i: (b, i, h)), state],
        out_shape=[jax.ShapeDtypeStruct((B, T, n_heads * dv), BF16),
                   jax.ShapeDtypeStruct(s0.shape, F32)],
        scratch_shapes=[pltpu.VMEM((dk, dv), F32)],
        compiler_params=_cparams("parallel", "parallel", "arbitrary"),
        name="retention",
    )(proj, proj, proj, proj, cos, sin, intra, kdec, qdec, gch, gn_w, s0)


def _softplus(z):
    return jnp.maximum(z, 0.0) + jnp.log1p(jnp.exp(-jnp.abs(z)))


def _expm1(x):
    e = jnp.exp(x)
    em1 = e - 1.0
    return jnp.where(e == 1.0, x, jnp.where(em1 == -1.0, -1.0, em1 * x / jnp.log(e)))


def _gelu_tanh(x):
    return 0.5 * x * (1.0 + jnp.tanh(math.sqrt(2.0 / math.pi) * (x + 0.044715 * (x * x * x))))


def _rglru_kernel(x_ref, g_ref, cw_ref, cb_ref, wa_ref, ba_ref, wx_ref, bx_ref, lam_ref, h0_ref, c0_ref,
                  o_ref, hn_ref, cn_ref, xp_scr, a_scr, u_scr, h_scr, hc_scr, *, tb, conv_w, n_blocks):
    i = pl.program_id(1)
    hist = conv_w - 1
    lo = CONV_PAD - hist

    @pl.when(i == 0)
    def _():
        xp_scr[lo:CONV_PAD, :] = c0_ref[...]
        hc_scr[...] = h0_ref[...]

    @pl.when(i > 0)
    def _():
        xp_scr[lo:CONV_PAD, :] = xp_scr[tb + lo:tb + CONV_PAD, :]

    xp_scr[CONV_PAD:CONV_PAD + tb, :] = x_ref[...]
    xc = cb_ref[...] + xp_scr[lo:lo + tb, :] * cw_ref[0:1, :]
    for j in range(1, conv_w):
        xc = xc + xp_scr[lo + j:lo + j + tb, :] * cw_ref[j:j + 1, :]

    sp = RGLRU_C * _softplus(-lam_ref[...])
    bw = xc.shape[1] // n_blocks
    for n in range(n_blocks):
        cols = slice(n * bw, (n + 1) * bw)
        xb = xc[:, cols]
        xb16 = xb.astype(BF16)
        r = _sigmoid(_dot(xb16, wa_ref[n]) + ba_ref[:, cols])
        ig = _sigmoid(_dot(xb16, wx_ref[n]) + bx_ref[:, cols])
        log_a = -(r * sp[:, cols])
        a_scr[:, cols] = jnp.exp(log_a)
        u_scr[:, cols] = jnp.sqrt(-_expm1(2.0 * log_a)) * (ig * xb)

    def step(t, h):
        h = a_scr[pl.ds(t, 1), :] * h + u_scr[pl.ds(t, 1), :]
        h_scr[pl.ds(t, 1), :] = h
        return h

    h_last = lax.fori_loop(0, tb, step, hc_scr[...], unroll=8)
    hc_scr[...] = h_last
    o_ref[...] = (_gelu_tanh(g_ref[...]) * h_scr[...]).astype(o_ref.dtype)

    @pl.when(i == pl.num_programs(1) - 1)
    def _():
        hn_ref[...] = h_last
        cn_ref[...] = xp_scr[tb + lo:tb + CONV_PAD, :]


def rglru_mixer(proj, x_group, h0, conv0, conv_w, conv_b, w_a, b_a, w_x, b_x, lam):
    B, T, _ = proj.shape
    d_rnn = h0.shape[-1]
    cw = conv_w.shape[0]
    n_blocks = w_a.shape[0]
    tb = _pick(T, (256, 128, 64, 32, 16, 8))
    kern = functools.partial(_rglru_kernel, tb=tb, conv_w=cw, n_blocks=n_blocks)
    full = lambda *s: pl.BlockSpec(s, lambda b, i: (0,) * len(s))
    return pl.pallas_call(
        kern,
        grid=(B, T // tb),
        in_specs=[pl.BlockSpec((None, tb, d_rnn), lambda b, i: (b, i, x_group)),
                  pl.BlockSpec((None, tb, d_rnn), lambda b, i: (b, i, x_group + 1)),
                  full(cw, d_rnn), full(1, d_rnn),
                  full(*w_a.shape), full(1, d_rnn), full(*w_x.shape), full(1, d_rnn), full(1, d_rnn),
                  pl.BlockSpec((None, 1, d_rnn), lambda b, i: (b, 0, 0)),
                  pl.BlockSpec((None, cw - 1, d_rnn), lambda b, i: (b, 0, 0))],
        out_specs=[pl.BlockSpec((None, tb, d_rnn), lambda b, i: (b, i, 0)),
                   pl.BlockSpec((None, 1, d_rnn), lambda b, i: (b, 0, 0)),
                   pl.BlockSpec((None, cw - 1, d_rnn), lambda b, i: (b, 0, 0))],
        out_shape=[jax.ShapeDtypeStruct((B, T, d_rnn), BF16),
                   jax.ShapeDtypeStruct((B, 1, d_rnn), F32),
                   jax.ShapeDtypeStruct((B, cw - 1, d_rnn), F32)],
        scratch_shapes=[pltpu.VMEM((tb + CONV_PAD, d_rnn), F32), pltpu.VMEM((tb, d_rnn), F32),
                        pltpu.VMEM((tb, d_rnn), F32), pltpu.VMEM((tb, d_rnn), F32),
                        pltpu.VMEM((1, d_rnn), F32)],
        compiler_params=_cparams("parallel", "arbitrary"),
        name="rglru",
    )(proj, proj, conv_w, conv_b, w_a, b_a, w_x, b_x, lam, h0, conv0)


def _merge_kernel(pr_ref, pn_ref, wr_ref, wn_ref, g0_ref, g1_ref, o_ref):
    yr = _dot(pr_ref[...], wr_ref[...])
    yn = _dot(pn_ref[...], wn_ref[...])
    o_ref[...] = (_sigmoid(g0_ref[...]) * yr + _sigmoid(g1_ref[...]) * yn).astype(o_ref.dtype)


def merge_branches(pre_ret, pre_rnn, w_ret, w_rnn, proj, gate_off):
    M, kr = pre_ret.shape
    kn = pre_rnn.shape[1]
    D = w_ret.shape[1]
    tm = _pick(M, (1024, 512, 256, 128, 64, 32, 16, 8))
    tn = _pick(math.gcd(D, gate_off), (512, 256, 128))
    g0 = gate_off // tn
    g1 = (gate_off + D) // tn
    return pl.pallas_call(
        _merge_kernel,
        grid=(M // tm, D // tn),
        in_specs=[pl.BlockSpec((tm, kr), lambda i, j: (i, 0)),
                  pl.BlockSpec((tm, kn), lambda i, j: (i, 0)),
                  pl.BlockSpec((kr, tn), lambda i, j: (0, j)),
                  pl.BlockSpec((kn, tn), lambda i, j: (0, j)),
                  pl.BlockSpec((tm, tn), lambda i, j: (i, g0 + j)),
                  pl.BlockSpec((tm, tn), lambda i, j: (i, g1 + j))],
        out_specs=pl.BlockSpec((tm, tn), lambda i, j: (i, j)),
        out_shape=jax.ShapeDtypeStruct((M, D), BF16),
        compiler_params=_cparams("parallel", "parallel"),
        name="merge_branches",
    )(pre_ret, pre_rnn, w_ret, w_rnn, proj, proj)


def _pack_pair(x):
    half = x.shape[-1] // 2
    lo = lax.bitcast_convert_type(x[:, :half].astype(BF16).astype(F32), jnp.uint32) >> 16
    hi = lax.bitcast_convert_type(x[:, half:].astype(BF16).astype(F32), jnp.uint32) & jnp.uint32(0xFFFF0000)
    return lo | hi


def _unpack_pair(w):
    lo = lax.bitcast_convert_type(w << 16, F32)
    hi = lax.bitcast_convert_type(w & jnp.uint32(0xFFFF0000), F32)
    return lo, hi


def _post1_kernel(x_ref, mix_ref, g1_ref, sc_ref, sh_ref, w_ref, b_ref, x1_ref, h2_ref, *, alpha):
    x1 = _ln(alpha * x_ref[...] + g1_ref[...] * mix_ref[...]) * w_ref[...] + b_ref[...]
    x1_ref[...] = x1
    h2_ref[...] = _pack_pair(_ln(x1) * (1.0 + sc_ref[...]) + sh_ref[...])


def post_mixer(x, mix, g1, sc2, sh2, ln_w, ln_b, alpha):
    G, R, D = x.shape
    tm = _pick(R, (256, 128, 64, 32, 16, 8))
    row = pl.BlockSpec((None, tm, D), lambda g, i: (g, i, 0))
    vec = pl.BlockSpec((1, D), lambda g, i: (0, 0))
    return pl.pallas_call(
        functools.partial(_post1_kernel, alpha=alpha),
        grid=(G, R // tm),
        in_specs=[row, row, _mod_spec(g1, tm), _mod_spec(sc2, tm), _mod_spec(sh2, tm), vec, vec],
        out_specs=[row, pl.BlockSpec((None, tm, D // 2), lambda g, i: (g, i, 0))],
        out_shape=[jax.ShapeDtypeStruct((G, R, D), F32), jax.ShapeDtypeStruct((G, R, D // 2), jnp.uint32)],
        compiler_params=_cparams("parallel", "parallel"),
        name="post_mixer",
    )(x, mix, g1, sc2, sh2, ln_w, ln_b)


def _router_kernel(h_ref, wlo_ref, whi_ref, b_ref, idx_ref, wt_ref, rank_ref, cnt_ref, carry_scr, *, n_experts):
    tm = h_ref.shape[0]
    gsz = n_experts // N_GROUPS
    neg = -jnp.inf

    @pl.when(pl.program_id(0) == 0)
    def _():
        carry_scr[...] = jnp.zeros_like(carry_scr)

    h_lo, h_hi = _unpack_pair(h_ref[...])
    s = _sigmoid(_dot(h_lo.astype(BF16), wlo_ref[...]) + _dot(h_hi.astype(BF16), whi_ref[...]))
    sb = s + b_ref[...]
    lane = lax.broadcasted_iota(jnp.int32, (tm, n_experts), 1).astype(F32)
    grp = jnp.zeros((tm, n_experts), F32)
    for g in range(1, N_GROUPS):
        grp = grp + (lane >= float(g * gsz)).astype(F32)
    lane_o = lax.broadcasted_iota(jnp.int32, (tm, LANES), 1).astype(F32)

    def argmax_first(v, ids, big):
        m = jnp.max(v, axis=-1, keepdims=True)
        first = jnp.min(jnp.where(v == m, ids, float(big)), axis=-1, keepdims=True)
        return m, first

    gscore = jnp.full((tm, LANES), neg, F32)
    for g in range(N_GROUPS):
        vg = jnp.where(grp == g, sb, neg)
        m1, i1 = argmax_first(vg, lane, n_experts)
        m2 = jnp.max(jnp.where(lane == i1, neg, vg), axis=-1, keepdims=True)
        gscore = jnp.where(lane_o == g, m1 + m2, gscore)
    keep = jnp.zeros((tm, n_experts), jnp.bool_)
    for _ in range(TOPK_GROUPS):
        _, gi = argmax_first(gscore, lane_o, LANES)
        gscore = jnp.where(lane_o == gi, neg, gscore)
        keep = jnp.logical_or(keep, grp == gi)
    sel = jnp.where(keep, sb, neg)
    idx_o = jnp.zeros((tm, LANES), F32)
    wt_o = jnp.zeros((tm, LANES), F32)
    chosen = []
    taken = jnp.zeros((tm, n_experts), F32)
    for k in range(TOP_K):
        _, ei = argmax_first(sel, lane, n_experts)
        hit = lane == ei
        sel = jnp.where(hit, neg, sel)
        wk = jnp.sum(jnp.where(hit, s, 0.0), axis=-1, keepdims=True)
        idx_o = jnp.where(lane_o == k, ei, idx_o)
        wt_o = jnp.where(lane_o == k, wk, wt_o)
        taken = jnp.where(hit, 1.0, taken)
        chosen.append(ei)
    wt_o = wt_o / jnp.sum(wt_o, axis=-1, keepdims=True) * ROUTED_SCALE
    idx_ref[...] = idx_o.astype(jnp.int32)
    wt_ref[...] = wt_o
    ri = lax.broadcasted_iota(jnp.int32, (tm, tm), 0)
    ci = lax.broadcasted_iota(jnp.int32, (tm, tm), 1)
    before = _dot((ri > ci).astype(BF16), taken.astype(BF16)) + carry_scr[...]
    rank_o = jnp.zeros((tm, LANES), F32)
    for k in range(TOP_K):
        rk = jnp.sum(jnp.where(lane == chosen[k], before, 0.0), axis=-1, keepdims=True)
        rank_o = jnp.where(lane_o == k, rk, rank_o)
    rank_ref[...] = rank_o.astype(jnp.int32)
    carry_scr[...] = carry_scr[...] + jnp.sum(taken, axis=0, keepdims=True)
    cnt_ref[...] = carry_scr[...]


def router(h2p, w_router, bias):
    M, H = h2p.shape
    E = w_router.shape[1]
    tm = _pick(M, (256, 128, 64, 32, 16, 8))
    out = pl.BlockSpec((tm, LANES), lambda i: (i, 0))
    return pl.pallas_call(
        functools.partial(_router_kernel, n_experts=E),
        grid=(M // tm,),
        in_specs=[pl.BlockSpec((tm, H), lambda i: (i, 0)),
                  pl.BlockSpec((H, E), lambda i: (0, 0)),
                  pl.BlockSpec((H, E), lambda i: (1, 0)),
                  pl.BlockSpec((1, E), lambda i: (0, 0))],
        out_specs=[out, out, out, pl.BlockSpec((1, E), lambda i: (0, 0))],
        out_shape=[jax.ShapeDtypeStruct((M, LANES), jnp.int32), jax.ShapeDtypeStruct((M, LANES), F32),
                   jax.ShapeDtypeStruct((M, LANES), jnp.int32), jax.ShapeDtypeStruct((1, E), F32)],
        scratch_shapes=[pltpu.VMEM((1, E), F32)],
        compiler_params=_cparams("arbitrary"),
        name="router",
    )(h2p, w_router, w_router, bias)


def _row_copy(src, src_row, dst, dst_row, sem):
    return pltpu.make_async_copy(src.at[pl.ds(src_row, 1)], dst.at[pl.ds(dst_row, 1)], sem)


def _pos_copy(pos_hbm, pos_smem, isem, tile, s, n_rows):
    start = pl.multiple_of(s * n_rows, n_rows)
    return pltpu.make_async_copy(pos_hbm.at[tile], pos_smem.at[pl.ds(start, n_rows)], isem.at[s])


def _pad_chunks(rows):
    n, out = rows // 2, []
    while n >= SUBLANES:
        out.append(n)
        n //= 2
    return out


def _dispatch_kernel(pstart_ref, count_ref, padded_ref, nused_ref, pos_hbm, h_ref, xs_hbm,
                     pos_smem, xbuf, zbuf, isem, dsem, zsem, *, tm, n_experts, bm):
    i = pl.program_id(0)
    nt = pl.num_programs(0)
    slot = i % 2
    nxt = 1 - slot
    n_rows = tm * TOP_K

    def pad_fill(wait):
        def body(e, c):
            npad = padded_ref[e] - count_ref[e]
            off = pstart_ref[e] + count_ref[e]
            n_single = (-count_ref[e]) & (SUBLANES - 1)
            for j in range(SUBLANES - 1):
                cp = _row_copy(zbuf, 0, xs_hbm, off + j, zsem)

                @pl.when(j < n_single)
                def _():
                    cp.wait() if wait else cp.start()

            off = pl.multiple_of(off + n_single, SUBLANES)
            n_tiled = npad - n_single
            for n in _pad_chunks(bm):
                cp = pltpu.make_async_copy(zbuf.at[pl.ds(0, n)], xs_hbm.at[pl.ds(off, n)], zsem)

                @pl.when((n_tiled & n) != 0)
                def _():
                    cp.wait() if wait else cp.start()

                off = pl.multiple_of(off + (n_tiled & n), SUBLANES)
            return c
        lax.fori_loop(0, n_experts, body, 0)

        def tail(b, c):
            for part in range(2):
                row0 = pl.multiple_of(b * bm + part * (bm // 2), SUBLANES)
                cp = pltpu.make_async_copy(zbuf, xs_hbm.at[pl.ds(row0, bm // 2)], zsem)
                cp.wait() if wait else cp.start()
            return c
        lax.fori_loop(nused_ref[0], xs_hbm.shape[0] // bm, tail, 0)

    @pl.when(i == 0)
    def _():
        zbuf[...] = jnp.zeros_like(zbuf)
        pad_fill(False)
        _pos_copy(pos_hbm, pos_smem, isem, 0, 0, n_rows).start()
        _pos_copy(pos_hbm, pos_smem, isem, 0, 0, n_rows).wait()
        pad_fill(True)

    @pl.when(i + 1 < nt)
    def _():
        _pos_copy(pos_hbm, pos_smem, isem, i + 1, nxt, n_rows).start()

    xbuf[slot] = h_ref[...]
    base = slot * n_rows

    for r in range(tm):
        for k in range(TOP_K):
            _row_copy(xbuf.at[slot], r, xs_hbm, pos_smem[base + r * TOP_K + k],
                      dsem.at[slot]).start(priority=k % 2)

    def wait_all(s):
        def body(r, c):
            _row_copy(xbuf.at[s], 0, xs_hbm, 0, dsem.at[s]).wait()
            return c
        lax.fori_loop(0, n_rows, body, 0, unroll=8)

    @pl.when(i > 0)
    def _():
        wait_all(nxt)

    @pl.when(i + 1 < nt)
    def _():
        _pos_copy(pos_hbm, pos_smem, isem, i + 1, nxt, n_rows).wait()

    @pl.when(i == nt - 1)
    def _():
        wait_all(slot)


def dispatch_tokens(h2p, pos_tiles, pstart, counts, padded, n_used, n_slots, bm):
    M, H = h2p.shape
    nt, n_rows = pos_tiles.shape
    tm = n_rows // TOP_K
    any_spec = pl.BlockSpec(memory_space=pl.ANY)
    grid_spec = pltpu.PrefetchScalarGridSpec(
        num_scalar_prefetch=4,
        grid=(nt,),
        in_specs=[any_spec, pl.BlockSpec((tm, H), lambda i, *_: (i, 0))],
        out_specs=any_spec,
        scratch_shapes=[pltpu.SMEM((2 * n_rows,), jnp.int32), pltpu.VMEM((2, tm, H), jnp.uint32),
                        pltpu.VMEM((bm // 2, H), jnp.uint32),
                        pltpu.SemaphoreType.DMA((2,)), pltpu.SemaphoreType.DMA((2,)),
                        pltpu.SemaphoreType.DMA],
    )
    return pl.pallas_call(
        functools.partial(_dispatch_kernel, tm=tm, n_experts=pstart.shape[0], bm=bm),
        grid_spec=grid_spec,
        out_shape=jax.ShapeDtypeStruct((n_slots, H), jnp.uint32),
        compiler_params=_cparams("arbitrary"),
        name="dispatch",
    )(pstart, counts, padded, n_used, pos_tiles, h2p)


def _experts_kernel(blk_e_ref, first_ref, next_ref, nused_ref, xs_ref, w1_hbm, w3_hbm, w2_hbm, y_ref,
                    st1, st3, st2, wb1, wb3, wb2, wsem):
    i = pl.program_id(0)
    half = xs_ref.shape[1]

    mats = ((w1_hbm, st1, wb1), (w3_hbm, st3, wb3), (w2_hbm, st2, wb2))

    def weight_copy(m, e):
        return pltpu.make_async_copy(mats[m][0].at[e], mats[m][1], wsem.at[m])

    def block(refresh):
        def load(m):
            if refresh:
                mats[m][2][...] = mats[m][1][...].astype(BF16)
                weight_copy(m, next_ref[i]).start()

        x_lo, x_hi = _unpack_pair(xs_ref[...])
        x_lo = x_lo.astype(BF16)
        x_hi = x_hi.astype(BF16)
        load(0)
        h1 = _dot(x_lo, wb1[0:half, :]) + _dot(x_hi, wb1[half:2 * half, :])
        load(1)
        h3 = _dot(x_lo, wb3[0:half, :]) + _dot(x_hi, wb3[half:2 * half, :])
        hh = (h1 * _sigmoid(h1) * h3).astype(BF16)
        load(2)
        y_ref[...] = _pack_pair(_dot(hh, wb2[...]))

    @pl.when(i == 0)
    def _():
        for m in range(3):
            weight_copy(m, blk_e_ref[0]).start()

    @pl.when(first_ref[i] == 1)
    def _():
        for m in range(3):
            weight_copy(m, blk_e_ref[i]).wait()
        block(True)

    @pl.when((first_ref[i] == 0) & (i < nused_ref[0]))
    def _():
        block(False)

    @pl.when(i >= nused_ref[0])
    def _():
        y_ref[...] = jnp.zeros_like(y_ref)

    @pl.when(i == pl.num_programs(0) - 1)
    def _():
        for m in range(3):
            weight_copy(m, 0).wait()


def routed_expert_blocks(xs, blk_e, first, next_e, n_used, w1, w3, w2, bm):
    n_slots, H = xs.shape
    nb = n_slots // bm
    _, D, F = w1.shape
    any_spec = pl.BlockSpec(memory_space=pl.ANY)
    grid_spec = pltpu.PrefetchScalarGridSpec(
        num_scalar_prefetch=4,
        grid=(nb,),
        in_specs=[pl.BlockSpec((bm, H), lambda i, be, fi, ne, nu: (jnp.minimum(i, nu[0] - 1), 0)),
                  any_spec, any_spec, any_spec],
        out_specs=pl.BlockSpec((bm, H), lambda i, be, fi, ne, nu: (i, 0)),
        scratch_shapes=[pltpu.VMEM((D, F), F32), pltpu.VMEM((D, F), F32), pltpu.VMEM((F, D), F32),
                        pltpu.VMEM((D, F), BF16), pltpu.VMEM((D, F), BF16), pltpu.VMEM((F, D), BF16),
                        pltpu.SemaphoreType.DMA((3,))],
    )
    return pl.pallas_call(
        _experts_kernel,
        grid_spec=grid_spec,
        out_shape=jax.ShapeDtypeStruct((n_slots, H), jnp.uint32),
        compiler_params=_cparams("arbitrary"),
        name="routed_experts",
    )(blk_e, first, next_e, n_used, xs, w1, w3, w2)


def _final_kernel(pos_hbm, y_hbm, x1_ref, h2_ref, wt_ref, g2_ref, ws1_ref, ws3_ref, ws2_ref,
                  w_ref, b_ref, o_ref, ybuf, pos_smem, gsem, isem, *, tm, alpha, tile_off):
    g = pl.program_id(0)
    i = pl.program_id(1)
    nt = pl.num_programs(1)
    step = g * nt + i
    nsteps = pl.num_programs(0) * nt
    slot = step % 2
    nxt = 1 - slot
    n_rows = tm * TOP_K
    half = h2_ref.shape[1]

    def pos_copy(st, s):
        return _pos_copy(pos_hbm, pos_smem, isem, tile_off + st, s, n_rows)

    def issue_rows(s):
        base = s * n_rows
        for r in range(tm):
            for k in range(TOP_K):
                _row_copy(y_hbm, pos_smem[base + r * TOP_K + k], ybuf.at[s], k * tm + r, gsem.at[s]).start()

    def wait_rows(s):
        def body(r, c):
            _row_copy(y_hbm, 0, ybuf.at[s], r, gsem.at[s]).wait()
            return c
        lax.fori_loop(0, n_rows, body, 0, unroll=8)

    @pl.when(step == 0)
    def _():
        pos_copy(0, 0).start()
        pos_copy(0, 0).wait()
        issue_rows(0)

        @pl.when(nsteps > 1)
        def _():
            pos_copy(1, 1).start()

    @pl.when(step + 1 < nsteps)
    def _():
        pos_copy(step + 1, nxt).wait()

    issue_rows(nxt)

    h_lo, h_hi = _unpack_pair(h2_ref[...])
    h_lo = h_lo.astype(BF16)
    h_hi = h_hi.astype(BF16)
    a1 = _dot(h_lo, ws1_ref[0:half, :]) + _dot(h_hi, ws1_ref[half:2 * half, :])
    a3 = _dot(h_lo, ws3_ref[0:half, :]) + _dot(h_hi, ws3_ref[half:2 * half, :])
    shared = _dot((a1 * _sigmoid(a1) * a3).astype(BF16), ws2_ref[...])
    ffn_lo = shared[:, :half]
    ffn_hi = shared[:, half:]

    wait_rows(slot)

    @pl.when(step + 2 < nsteps)
    def _():
        pos_copy(step + 2, slot).start()

    wt = wt_ref[...]
    for k in range(TOP_K):
        y_lo, y_hi = _unpack_pair(ybuf[slot, k * tm:(k + 1) * tm, :])
        wk = wt[:, k:k + 1]
        ffn_lo = ffn_lo + wk * y_lo
        ffn_hi = ffn_hi + wk * y_hi
    ffn = jnp.concatenate([ffn_lo, ffn_hi], axis=-1)
    o_ref[...] = _ln(alpha * x1_ref[...] + g2_ref[...] * ffn) * w_ref[...] + b_ref[...]

    @pl.when(step == nsteps - 1)
    def _():
        wait_rows(nxt)


def combine_and_finish(x1, h2p, wts, pos_tiles, tile_off, y, g2, ws1, ws3, ws2, ln_w, ln_b, alpha):
    G, R, D = x1.shape
    H = D // 2
    n_rows = pos_tiles.shape[1]
    tm = n_rows // TOP_K
    F = ws1.shape[1]
    assert G * (R // tm) >= 2, "the row prefetch needs at least two token tiles"
    any_spec = pl.BlockSpec(memory_space=pl.ANY)
    row = lambda w: pl.BlockSpec((None, tm, w), lambda g, i: (g, i, 0))
    const = lambda r, c: pl.BlockSpec((r, c), lambda g, i: (0, 0))
    kern = functools.partial(_final_kernel, tm=tm, alpha=alpha, tile_off=tile_off)
    return pl.pallas_call(
        kern,
        grid=(G, R // tm),
        in_specs=[any_spec, any_spec, row(D), row(H), row(LANES), _mod_spec(g2, tm),
                  const(D, F), const(D, F), const(F, D), const(1, D), const(1, D)],
        out_specs=row(D),
        out_shape=jax.ShapeDtypeStruct((G, R, D), F32),
        scratch_shapes=[pltpu.VMEM((2, n_rows, H), jnp.uint32), pltpu.SMEM((2 * n_rows,), jnp.int32),
                        pltpu.SemaphoreType.DMA((2,)), pltpu.SemaphoreType.DMA((2,))],
        compiler_params=_cparams("arbitrary", "arbitrary"),
        name="combine_finish",
    )(pos_tiles, y, x1, h2p, wts, g2, ws1, ws3, ws2, ln_w, ln_b)


def _slots_kernel(idx_ref, rank_ref, pstart_ref, pos_ref):
    tm = idx_ref.shape[0]
    E = pstart_ref.shape[1]
    lane = lax.broadcasted_iota(jnp.int32, (tm, E), 1)
    lane_o = lax.broadcasted_iota(jnp.int32, (tm, LANES), 1)
    idx = idx_ref[...]
    ps = pstart_ref[...]
    start = jnp.zeros((tm, LANES), F32)
    for k in range(TOP_K):
        sk = jnp.sum(jnp.where(lane == idx[:, k:k + 1], ps, 0.0), axis=-1, keepdims=True)
        start = jnp.where(lane_o == k, sk, start)
    pos_ref[...] = start.astype(jnp.int32) + rank_ref[...]


def assignment_slots(idx, rank, pstart):
    M = idx.shape[0]
    E = pstart.shape[1]
    tm = _pick(M, (256, 128, 64, 32, 16, 8))
    blk = pl.BlockSpec((tm, LANES), lambda i: (i, 0))
    return pl.pallas_call(
        _slots_kernel,
        grid=(M // tm,),
        in_specs=[blk, blk, pl.BlockSpec((1, E), lambda i: (0, 0))],
        out_specs=blk,
        out_shape=jax.ShapeDtypeStruct((M, LANES), jnp.int32),
        compiler_params=_cparams("parallel"),
        name="assignment_slots",
    )(idx, rank, pstart)


def _block_tables(counts, n_blocks, bm):
    E = counts.shape[0]
    padded = (counts + bm - 1) // bm * bm
    pend = jnp.cumsum(padded)
    pstart = pend - padded
    n_used = pend[-1] // bm
    blk = jnp.arange(n_blocks, dtype=jnp.int32)
    blk_e = jnp.minimum(jnp.sum((pend[None, :] <= (blk * bm)[:, None]).astype(jnp.int32), axis=1), E - 1)
    prev_e = jnp.concatenate([jnp.full((1,), -1, jnp.int32), blk_e[:-1]])
    first = ((blk_e != prev_e) & (blk < n_used)).astype(jnp.int32)
    ids = jnp.arange(E, dtype=jnp.int32)
    active = jnp.where(counts > 0, ids, E)
    later = lax.cummin(active[::-1])[::-1]
    nxt = jnp.concatenate([later[1:], jnp.full((1,), E, jnp.int32)])
    nxt = jnp.where(nxt >= E, later[0], nxt)
    next_e = jnp.sum(jnp.where(ids[None, :] == blk_e[:, None], nxt[None, :], 0), axis=1).astype(jnp.int32)
    i32 = lambda v: v.astype(jnp.int32)
    return i32(pstart), i32(padded), blk_e, first, next_e, i32(n_used).reshape(1)


def _assignment_tiles(v, tm):
    M = v.shape[0]
    return v[:, :TOP_K].reshape(M // tm, tm * TOP_K)


def _stream_mixers(x, mods, pos, s_ret, h_rnn, conv_buf, lw, alpha):
    B, T, D = x.shape
    n_heads, d_ret = lw["n_heads"], lw["d_ret"]
    d_rnn = lw["d_rnn"]
    per_row = mods["per_row"]
    xg = x.reshape(1, B * T, D) if per_row else x
    h = ln_mod(xg, mods["sc1"], mods["sh1"]).reshape(B * T, D)
    proj = matmul(h, lw["w_in"]).reshape(B, T, -1)
    pre_ret, s_new = retention_mixer(proj, pos, s_ret, lw["ret_gn_w"], n_heads, d_ret)
    x_group = (4 * d_ret) // d_rnn
    pre_rnn, h_new, conv_new = rglru_mixer(proj, x_group, h_rnn[:, None, :], conv_buf, lw["conv_w"],
                                           lw["conv_b"], lw["w_a"], lw["b_a"], lw["w_x"], lw["b_x"],
                                           lw["rg_lambda"])
    mixin = merge_branches(pre_ret.reshape(B * T, d_ret), pre_rnn.reshape(B * T, d_rnn),
                           lw["w_br_ret"], lw["w_br_rnn"], proj.reshape(B * T, -1), 4 * d_ret + 2 * d_rnn)
    mix = matmul(mixin, lw["w_o"])
    x1, h2 = post_mixer(xg, mix.reshape(xg.shape), mods["g1"], mods["sc2"], mods["sh2"],
                        lw["ln1_w"], lw["ln1_b"], alpha)
    return x1, h2, s_new, h_new[:, 0, :], conv_new


def _split_mods(mod, B, T, per_row):
    D = mod.shape[1] // 6
    parts = mod.reshape(B, 6, D)
    out = {"per_row": per_row}
    for n, name in enumerate(("sh1", "sc1", "g1", "sh2", "sc2", "g2")):
        p = parts[:, n, :]
        out[name] = jnp.repeat(p, T, axis=0)[None] if per_row else p[:, None, :]
    return out


def kernel(x_prompt, x_sample, state_ret, state_rglru, state_conv, c_prompt, c_sample, w_ada, b_ada, w_in, ret_gn_w, conv_w, conv_b, w_a, b_a, w_x, b_x, rg_lambda, w_br_ret, w_br_rnn, w_o, ln1_w, ln1_b, w_router, router_bias, w1, w3, w2, ws1, ws3, ws2, ln2_w, ln2_b):
    B, T, D = x_prompt.shape
    Bs, Ts, _ = x_sample.shape
    depth = w_ada.shape[0]
    n_heads, dk, dv = state_ret.shape[2:]
    d_ret = n_heads * dk
    d_rnn = state_rglru.shape[-1]
    cw = conv_w.shape[1]
    n_experts = w_router.shape[-1]
    alpha = (2.0 * depth) ** 0.25
    pos_p = jnp.arange(T, dtype=jnp.int32)
    pos_s = PAST_LEN + jnp.arange(Ts, dtype=jnp.int32)
    row = lambda v: v.reshape(1, -1)

    yp, ys = x_prompt, x_sample
    outs = [[] for _ in range(6)]
    for l in range(depth):
        lw = dict(n_heads=n_heads, d_ret=d_ret, d_rnn=d_rnn,
                  w_in=w_in[l].astype(BF16), ret_gn_w=row(ret_gn_w[l]), conv_w=conv_w[l], conv_b=row(conv_b[l]),
                  w_a=w_a[l].astype(BF16), b_a=row(b_a[l]), w_x=w_x[l].astype(BF16), b_x=row(b_x[l]),
                  rg_lambda=row(rg_lambda[l]), w_br_ret=w_br_ret[l].astype(BF16),
                  w_br_rnn=w_br_rnn[l].astype(BF16), w_o=w_o[l].astype(BF16),
                  ln1_w=row(ln1_w[l]), ln1_b=row(ln1_b[l]))
        mod = ada_mod(jnp.concatenate([c_prompt, c_sample], axis=0), w_ada[l], row(b_ada[l]))
        mods_p = _split_mods(mod[:B], B, T, per_row=False)
        mods_s = _split_mods(mod[B:], Bs, Ts, per_row=True)

        zero_ret = jnp.zeros((B, n_heads, dk, dv), F32)
        zero_h = jnp.zeros((B, d_rnn), F32)
        zero_conv = jnp.zeros((B, cw - 1, d_rnn), F32)
        x1p, h2p, r1, h1, c1 = _stream_mixers(yp, mods_p, pos_p, zero_ret, zero_h, zero_conv, lw, alpha)
        x1s, h2s, r2, hh2, c2 = _stream_mixers(ys, mods_s, pos_s, state_ret[l], state_rglru[l],
                                               state_conv[l], lw, alpha)

        Mp, Ms = B * T, Bs * Ts
        M = Mp + Ms
        bm = EXPERT_ROWS
        tm = math.gcd(math.gcd(T, Ms // 2), TOKEN_TILE)
        h2_all = jnp.concatenate([h2p.reshape(Mp, D // 2), h2s.reshape(Ms, D // 2)], axis=0)
        idx_l, wt_l, rank_l, cnt = router(h2_all, w_router[l].astype(BF16), row(router_bias[l]))
        n_blocks = -(-(M * TOP_K) // bm) + n_experts
        pstart, padded, blk_e, first, next_e, n_used = _block_tables(cnt[0].astype(jnp.int32), n_blocks, bm)
        pos_t = _assignment_tiles(assignment_slots(idx_l, rank_l, row(pstart).astype(F32)), tm)
        xs = dispatch_tokens(h2_all, pos_t, pstart, cnt[0].astype(jnp.int32), padded, n_used, n_blocks * bm, bm)
        y_slots = routed_expert_blocks(xs, blk_e, first, next_e, n_used, w1[l], w3[l], w2[l], bm)
        ws = (ws1[l].astype(BF16), ws3[l].astype(BF16), ws2[l].astype(BF16))
        yp = combine_and_finish(x1p, h2p, wt_l[:Mp].reshape(B, T, LANES), pos_t, 0,
                                y_slots, mods_p["g2"], *ws, row(ln2_w[l]), row(ln2_b[l]), alpha)
        ys = combine_and_finish(x1s, h2s, wt_l[Mp:].reshape(1, Ms, LANES), pos_t, Mp // tm,
                                y_slots, mods_s["g2"], *ws, row(ln2_w[l]), row(ln2_b[l]), alpha)
        ys = ys.reshape(Bs, Ts, D)
        for lst, val in zip(outs, (r1, h1, c1, r2, hh2, c2)):
            lst.append(val)
    return (yp, ys) + tuple(jnp.stack(o) for o in outs)
```

```python
import functools
import math

import jax
import jax.numpy as jnp
from jax import lax
from jax.experimental import pallas as pl
from jax.experimental.pallas import tpu as pltpu

CHUNK = 64
TOP_K = 8
N_GROUPS = 8
TOPK_GROUPS = 4
ROUTED_SCALE = 2.5
RGLRU_C = 8.0
PAST_LEN = 4096
ROPE_BASE = 10000.0
LN_EPS = 1e-5

F32 = jnp.float32
BF16 = jnp.bfloat16

V7X_VMEM_LIMIT_BYTES = 56 * 1024 * 1024
LANES = 128
SUBLANES = 8
EXPERT_ROWS = 256
TOKEN_TILE = 128
CONV_PAD = 8


def _cparams(*sem):
    return pltpu.CompilerParams(dimension_semantics=sem, vmem_limit_bytes=V7X_VMEM_LIMIT_BYTES)


def _pick(n, cands):
    for c in cands:
        if n % c == 0:
            return c
    return n


def _ln(x):
    mu = jnp.mean(x, axis=-1, keepdims=True)
    xc = x - mu
    var = jnp.mean(xc * xc, axis=-1, keepdims=True)
    return xc * lax.rsqrt(var + LN_EPS)


def _sigmoid(x):
    return jax.nn.sigmoid(x)


def _dot(a, b):
    return jnp.dot(a, b, preferred_element_type=F32)


def _ada_kernel(c_ref, w_ref, b_ref, o_ref):
    c = c_ref[...]
    s = (c * _sigmoid(c)).astype(BF16)
    o_ref[...] = _dot(s, w_ref[...].astype(BF16)) + b_ref[...]


def ada_mod(c, w, b):
    R, D = c.shape
    N = w.shape[1]
    tn = _pick(N, (512, 256, 128))
    return pl.pallas_call(
        _ada_kernel,
        grid=(N // tn,),
        in_specs=[pl.BlockSpec((R, D), lambda j: (0, 0)),
                  pl.BlockSpec((D, tn), lambda j: (0, j)),
                  pl.BlockSpec((1, tn), lambda j: (0, j))],
        out_specs=pl.BlockSpec((R, tn), lambda j: (0, j)),
        out_shape=jax.ShapeDtypeStruct((R, N), F32),
        compiler_params=_cparams("arbitrary"),
        name="ada_mod",
    )(c, w, b)


def _mod_spec(arr, tm):
    D = arr.shape[-1]
    if arr.shape[1] == 1:
        return pl.BlockSpec((None, 1, D), lambda g, i: (g, 0, 0))
    return pl.BlockSpec((None, tm, D), lambda g, i: (g, i, 0))


def _lnmod_kernel(x_ref, sc_ref, sh_ref, o_ref):
    y = _ln(x_ref[...]) * (1.0 + sc_ref[...]) + sh_ref[...]
    o_ref[...] = y.astype(o_ref.dtype)


def ln_mod(x, sc, sh):
    G, R, D = x.shape
    tm = _pick(R, (512, 256, 128, 64, 32, 16, 8))
    row = pl.BlockSpec((None, tm, D), lambda g, i: (g, i, 0))
    return pl.pallas_call(
        _lnmod_kernel,
        grid=(G, R // tm),
        in_specs=[row, _mod_spec(sc, tm), _mod_spec(sh, tm)],
        out_specs=row,
        out_shape=jax.ShapeDtypeStruct((G, R, D), BF16),
        compiler_params=_cparams("parallel", "parallel"),
        name="ln_mod",
    )(x, sc, sh)


def _mm_kernel(a_ref, w_ref, o_ref):
    o_ref[...] = _dot(a_ref[...], w_ref[...]).astype(o_ref.dtype)


def matmul(a, w, out_dtype=F32):
    M, K = a.shape
    N = w.shape[1]
    tm = _pick(M, (1024, 512, 256, 128, 64, 32, 16, 8))
    tn = _pick(N, (1024, 512, 256, 128))
    return pl.pallas_call(
        _mm_kernel,
        grid=(M // tm, N // tn),
        in_specs=[pl.BlockSpec((tm, K), lambda i, j: (i, 0)),
                  pl.BlockSpec((K, tn), lambda i, j: (0, j))],
        out_specs=pl.BlockSpec((tm, tn), lambda i, j: (i, j)),
        out_shape=jax.ShapeDtypeStruct((M, N), out_dtype),
        compiler_params=_cparams("parallel", "parallel"),
        name="matmul",
    )(a, w)


def _rot(x, cos, sin):
    half = x.shape[-1] // 2
    x1 = x[:, :half]
    x2 = x[:, half:]
    return jnp.concatenate([x1 * cos - x2 * sin, x1 * sin + x2 * cos], axis=-1)


def _retention_kernel(q_ref, k_ref, v_ref, g_ref, cos_ref, sin_ref, intra_ref, kdec_ref, qdec_ref,
                      gch_ref, gnw_ref, s0_ref, o_ref, sfin_ref, s_scr, *, chunk, n_sub, q_scale):
    i = pl.program_id(2)

    @pl.when(i == 0)
    def _():
        s_scr[...] = s0_ref[...]

    intra = intra_ref[...]
    kdec = kdec_ref[...]
    qdec = qdec_ref[...]
    gch = gch_ref[...]
    gnw = gnw_ref[...]
    for c in range(n_sub):
        rows = pl.ds(c * chunk, chunk)
        cos = cos_ref[rows, :]
        sin = sin_ref[rows, :]
        q = _rot(q_ref[rows, :], cos, sin) * q_scale
        k = _rot(k_ref[rows, :], cos, sin)
        v = v_ref[rows, :].astype(BF16)
        s_prev = s_scr[...]
        scores = lax.dot_general(q.astype(BF16), k.astype(BF16), (((1,), (1,)), ((), ())),
                                 preferred_element_type=F32) * intra
        o = _dot(scores.astype(BF16), v) + _dot((q * qdec).astype(BF16), s_prev.astype(BF16))
        kv = lax.dot_general((k * kdec).astype(BF16), v, (((0,), (0,)), ((), ())),
                             preferred_element_type=F32)
        s_scr[...] = gch * s_prev + kv
        on = _ln(o) * gnw
        g = g_ref[rows, :]
        o_ref[rows, :] = (g * _sigmoid(g) * on).astype(o_ref.dtype)

    @pl.when(i == pl.num_programs(2) - 1)
    def _():
        sfin_ref[...] = s_scr[...]


def retention_mixer(proj, pos, s0, gn_w, n_heads, d_ret):
    B, T, _ = proj.shape
    dk = d_ret // n_heads
    dv = s0.shape[-1]
    chunk = min(T, CHUNK)
    n_chunks = T // chunk
    n_sub = _pick(n_chunks, (8, 4, 2, 1))
    tb = n_sub * chunk
    half = dk // 2
    inv = ROPE_BASE ** (-jnp.arange(half, dtype=F32) / half)
    ang = pos.astype(F32)[:, None] * inv[None, :]
    cos, sin = jnp.cos(ang), jnp.sin(ang)
    log_g = jnp.log1p(-jnp.exp2(-5.0 - jnp.arange(n_heads, dtype=F32)))
    ci = jnp.arange(chunk, dtype=F32)
    intra = jnp.exp(jnp.abs(ci[:, None] - ci[None, :])[None] * log_g[:, None, None])
    kdec = jnp.broadcast_to(jnp.exp((chunk - 1 - ci)[None, :] * log_g[:, None])[:, :, None],
                            (n_heads, chunk, dk))
    qdec = jnp.broadcast_to(jnp.exp((ci + 1.0)[None, :] * log_g[:, None])[:, :, None],
                            (n_heads, chunk, dk))
    gch = jnp.broadcast_to(jnp.exp(chunk * log_g)[:, None, None], (n_heads, 1, dv))

    def col(group):
        return pl.BlockSpec((None, tb, dk), lambda b, h, i: (b, i, group * n_heads + h))

    tab = pl.BlockSpec((tb, half), lambda b, h, i: (i, 0))
    per_head = lambda r, c: pl.BlockSpec((None, r, c), lambda b, h, i: (h, 0, 0))
    state = pl.BlockSpec((None, None, dk, dv), lambda b, h, i: (b, h, 0, 0))
    kern = functools.partial(_retention_kernel, chunk=chunk, n_sub=n_sub, q_scale=dk ** -0.5)
    return pl.pallas_call(
        kern,
        grid=(B, n_heads, T // tb),
        in_specs=[col(0), col(1), col(2), col(3), tab, tab,
                  per_head(chunk, chunk), per_head(chunk, dk), per_head(chunk, dk), per_head(1, dv),
                  pl.BlockSpec((1, dv), lambda b, h, i: (0, h)), state],
        out_specs=[pl.BlockSpec((None, tb, dv), lambda b, h, i: (b, i, h)), state],
        out_shape=[jax.ShapeDtypeStruct((B, T, n_heads * dv), BF16),
                   jax.ShapeDtypeStruct(s0.shape, F32)],
        scratch_shapes=[pltpu.VMEM((dk, dv), F32)],
        compiler_params=_cparams("parallel", "parallel", "arbitrary"),
        name="retention",
    )(proj, proj, proj, proj, cos, sin, intra, kdec, qdec, gch, gn_w, s0)


def _softplus(z):
    return jnp.maximum(z, 0.0) + jnp.log1p(jnp.exp(-jnp.abs(z)))


def _expm1(x):
    e = jnp.exp(x)
    em1 = e - 1.0
    return jnp.where(e == 1.0, x, jnp.where(em1 == -1.0, -1.0, em1 * x / jnp.log(e)))


def _gelu_tanh(x):
    return 0.5 * x * (1.0 + jnp.tanh(math.sqrt(2.0 / math.pi) * (x + 0.044715 * (x * x * x))))


def _rglru_kernel(x_ref, g_ref, cw_ref, cb_ref, wa_ref, ba_ref, wx_ref, bx_ref, lam_ref, h0_ref, c0_ref,
                  o_ref, hn_ref, cn_ref, xp_scr, a_scr, u_scr, h_scr, hc_scr, *, tb, conv_w, n_blocks):
    i = pl.program_id(1)
    hist = conv_w - 1
    lo = CONV_PAD - hist

    @pl.when(i == 0)
    def _():
        xp_scr[lo:CONV_PAD, :] = c0_ref[...]
        hc_scr[...] = h0_ref[...]

    @pl.when(i > 0)
    def _():
        xp_scr[lo:CONV_PAD, :] = xp_scr[tb + lo:tb + CONV_PAD, :]

    xp_scr[CONV_PAD:CONV_PAD + tb, :] = x_ref[...]
    xc = cb_ref[...] + xp_scr[lo:lo + tb, :] * cw_ref[0:1, :]
    for j in range(1, conv_w):
        xc = xc + xp_scr[lo + j:lo + j + tb, :] * cw_ref[j:j + 1, :]

    sp = RGLRU_C * _softplus(-lam_ref[...])
    bw = xc.shape[1] // n_blocks
    for n in range(n_blocks):
        cols = slice(n * bw, (n + 1) * bw)
        xb = xc[:, cols]
        xb16 = xb.astype(BF16)
        r = _sigmoid(_dot(xb16, wa_ref[n]) + ba_ref[:, cols])
        ig = _sigmoid(_dot(xb16, wx_ref[n]) + bx_ref[:, cols])
        log_a = -(r * sp[:, cols])
        a_scr[:, cols] = jnp.exp(log_a)
        u_scr[:, cols] = jnp.sqrt(-_expm1(2.0 * log_a)) * (ig * xb)

    def step(t, h):
        h = a_scr[pl.ds(t, 1), :] * h + u_scr[pl.ds(t, 1), :]
        h_scr[pl.ds(t, 1), :] = h
        return h

    h_last = lax.fori_loop(0, tb, step, hc_scr[...], unroll=8)
    hc_scr[...] = h_last
    o_ref[...] = (_gelu_tanh(g_ref[...]) * h_scr[...]).astype(o_ref.dtype)

    @pl.when(i == pl.num_programs(1) - 1)
    def _():
        hn_ref[...] = h_last
        cn_ref[...] = xp_scr[tb + lo:tb + CONV_PAD, :]


def rglru_mixer(proj, x_group, h0, conv0, conv_w, conv_b, w_a, b_a, w_x, b_x, lam):
    B, T, _ = proj.shape
    d_rnn = h0.shape[-1]
    cw = conv_w.shape[0]
    n_blocks = w_a.shape[0]
    tb = _pick(T, (256, 128, 64, 32, 16, 8))
    kern = functools.partial(_rglru_kernel, tb=tb, conv_w=cw, n_blocks=n_blocks)
    full = lambda *s: pl.BlockSpec(s, lambda b, i: (0,) * len(s))
    return pl.pallas_call(
        kern,
        grid=(B, T // tb),
        in_specs=[pl.BlockSpec((None, tb, d_rnn), lambda b, i: (b, i, x_group)),
                  pl.BlockSpec((None, tb, d_rnn), lambda b, i: (b, i, x_group + 1)),
                  full(cw, d_rnn), full(1, d_rnn),
                  full(*w_a.shape), full(1, d_rnn), full(*w_x.shape), full(1, d_rnn), full(1, d_rnn),
                  pl.BlockSpec((None, 1, d_rnn), lambda b, i: (b, 0, 0)),
                  pl.BlockSpec((None, cw - 1, d_rnn), lambda b, i: (b, 0, 0))],
        out_specs=[pl.BlockSpec((None, tb, d_rnn), lambda b, i: (b, i, 0)),
                   pl.BlockSpec((None, 1, d_rnn), lambda b, i: (b, 0, 0)),
                   pl.BlockSpec((None, cw - 1, d_rnn), lambda b, i: (b, 0, 0))],
        out_shape=[jax.ShapeDtypeStruct((B, T, d_rnn), BF16),
                   jax.ShapeDtypeStruct((B, 1, d_rnn), F32),
                   jax.ShapeDtypeStruct((B, cw - 1, d_rnn), F32)],
        scratch_shapes=[pltpu.VMEM((tb + CONV_PAD, d_rnn), F32), pltpu.VMEM((tb, d_rnn), F32),
                        pltpu.VMEM((tb, d_rnn), F32), pltpu.VMEM((tb, d_rnn), F32),
                        pltpu.VMEM((1, d_rnn), F32)],
        compiler_params=_cparams("parallel", "arbitrary"),
        name="rglru",
    )(proj, proj, conv_w, conv_b, w_a, b_a, w_x, b_x, lam, h0, conv0)


def _merge_kernel(pr_ref, pn_ref, wr_ref, wn_ref, g0_ref, g1_ref, o_ref):
    yr = _dot(pr_ref[...], wr_ref[...])
    yn = _dot(pn_ref[...], wn_ref[...])
    o_ref[...] = (_sigmoid(g0_ref[...]) * yr + _sigmoid(g1_ref[...]) * yn).astype(o_ref.dtype)


def merge_branches(pre_ret, pre_rnn, w_ret, w_rnn, proj, gate_off):
    M, kr = pre_ret.shape
    kn = pre_rnn.shape[1]
    D = w_ret.shape[1]
    tm = _pick(M, (1024, 512, 256, 128, 64, 32, 16, 8))
    tn = _pick(math.gcd(D, gate_off), (512, 256, 128))
    g0 = gate_off // tn
    g1 = (gate_off + D) // tn
    return pl.pallas_call(
        _merge_kernel,
        grid=(M // tm, D // tn),
        in_specs=[pl.BlockSpec((tm, kr), lambda i, j: (i, 0)),
                  pl.BlockSpec((tm, kn), lambda i, j: (i, 0)),
                  pl.BlockSpec((kr, tn), lambda i, j: (0, j)),
                  pl.BlockSpec((kn, tn), lambda i, j: (0, j)),
                  pl.BlockSpec((tm, tn), lambda i, j: (i, g0 + j)),
                  pl.BlockSpec((tm, tn), lambda i, j: (i, g1 + j))],
        out_specs=pl.BlockSpec((tm, tn), lambda i, j: (i, j)),
        out_shape=jax.ShapeDtypeStruct((M, D), BF16),
        compiler_params=_cparams("parallel", "parallel"),
        name="merge_branches",
    )(pre_ret, pre_rnn, w_ret, w_rnn, proj, proj)


def _pack_pair(x):
    half = x.shape[-1] // 2
    lo = lax.bitcast_convert_type(x[:, :half].astype(BF16).astype(F32), jnp.uint32) >> 16
    hi = lax.bitcast_convert_type(x[:, half:].astype(BF16).astype(F32), jnp.uint32) & jnp.uint32(0xFFFF0000)
    return lo | hi


def _unpack_pair(w):
    lo = lax.bitcast_convert_type(w << 16, F32)
    hi = lax.bitcast_convert_type(w & jnp.uint32(0xFFFF0000), F32)
    return lo, hi


def _post1_kernel(x_ref, mix_ref, g1_ref, sc_ref, sh_ref, w_ref, b_ref, x1_ref, h2_ref, *, alpha):
    x1 = _ln(alpha * x_ref[...] + g1_ref[...] * mix_ref[...]) * w_ref[...] + b_ref[...]
    x1_ref[...] = x1
    h2_ref[...] = _pack_pair(_ln(x1) * (1.0 + sc_ref[...]) + sh_ref[...])


def post_mixer(x, mix, g1, sc2, sh2, ln_w, ln_b, alpha):
    G, R, D = x.shape
    tm = _pick(R, (256, 128, 64, 32, 16, 8))
    row = pl.BlockSpec((None, tm, D), lambda g, i: (g, i, 0))
    vec = pl.BlockSpec((1, D), lambda g, i: (0, 0))
    return pl.pallas_call(
        functools.partial(_post1_kernel, alpha=alpha),
        grid=(G, R // tm),
        in_specs=[row, row, _mod_spec(g1, tm), _mod_spec(sc2, tm), _mod_spec(sh2, tm), vec, vec],
        out_specs=[row, pl.BlockSpec((None, tm, D // 2), lambda g, i: (g, i, 0))],
        out_shape=[jax.ShapeDtypeStruct((G, R, D), F32), jax.ShapeDtypeStruct((G, R, D // 2), jnp.uint32)],
        compiler_params=_cparams("parallel", "parallel"),
        name="post_mixer",
    )(x, mix, g1, sc2, sh2, ln_w, ln_b)


def _router_kernel(h_ref, wlo_ref, whi_ref, b_ref, idx_ref, wt_ref, rank_ref, cnt_ref, carry_scr, *, n_experts):
    tm = h_ref.shape[0]
    gsz = n_experts // N_GROUPS
    neg = -jnp.inf

    @pl.when(pl.program_id(0) == 0)
    def _():
        carry_scr[...] = jnp.zeros_like(carry_scr)

    h_lo, h_hi = _unpack_pair(h_ref[...])
    s = _sigmoid(_dot(h_lo.astype(BF16), wlo_ref[...]) + _dot(h_hi.astype(BF16), whi_ref[...]))
    sb = s + b_ref[...]
    lane = lax.broadcasted_iota(jnp.int32, (tm, n_experts), 1).astype(F32)
    grp = jnp.zeros((tm, n_experts), F32)
    for g in range(1, N_GROUPS):
        grp = grp + (lane >= float(g * gsz)).astype(F32)
    lane_o = lax.broadcasted_iota(jnp.int32, (tm, LANES), 1).astype(F32)

    def argmax_first(v, ids, big):
        m = jnp.max(v, axis=-1, keepdims=True)
        first = jnp.min(jnp.where(v == m, ids, float(big)), axis=-1, keepdims=True)
        return m, first

    gscore = jnp.full((tm, LANES), neg, F32)
    for g in range(N_GROUPS):
        vg = jnp.where(grp == g, sb, neg)
        m1, i1 = argmax_first(vg, lane, n_experts)
        m2 = jnp.max(jnp.where(lane == i1, neg, vg), axis=-1, keepdims=True)
        gscore = jnp.where(lane_o == g, m1 + m2, gscore)
    keep = jnp.zeros((tm, n_experts), jnp.bool_)
    for _ in range(TOPK_GROUPS):
        _, gi = argmax_first(gscore, lane_o, LANES)
        gscore = jnp.where(lane_o == gi, neg, gscore)
        keep = jnp.logical_or(keep, grp == gi)
    sel = jnp.where(keep, sb, neg)
    idx_o = jnp.zeros((tm, LANES), F32)
    wt_o = jnp.zeros((tm, LANES), F32)
    chosen = []
    taken = jnp.zeros((tm, n_experts), F32)
    for k in range(TOP_K):
        _, ei = argmax_first(sel, lane, n_experts)
        hit = lane == ei
        sel = jnp.where(hit, neg, sel)
        wk = jnp.sum(jnp.where(hit, s, 0.0), axis=-1, keepdims=True)
        idx_o = jnp.where(lane_o == k, ei, idx_o)
        wt_o = jnp.where(lane_o == k, wk, wt_o)
        taken = jnp.where(hit, 1.0, taken)
        chosen.append(ei)
    wt_o = wt_o / jnp.sum(wt_o, axis=-1, keepdims=True) * ROUTED_SCALE
    idx_ref[...] = idx_o.astype(jnp.int32)
    wt_ref[...] = wt_o
    ri = lax.broadcasted_iota(jnp.int32, (tm, tm), 0)
    ci = lax.broadcasted_iota(jnp.int32, (tm, tm), 1)
    before = _dot((ri > ci).astype(BF16), taken.astype(BF16)) + carry_scr[...]
    rank_o = jnp.zeros((tm, LANES), F32)
    for k in range(TOP_K):
        rk = jnp.sum(jnp.where(lane == chosen[k], before, 0.0), axis=-1, keepdims=True)
        rank_o = jnp.where(lane_o == k, rk, rank_o)
    rank_ref[...] = rank_o.astype(jnp.int32)
    carry_scr[...] = carry_scr[...] + jnp.sum(taken, axis=0, keepdims=True)
    cnt_ref[...] = carry_scr[...]


def router(h2p, w_router, bias):
    M, H = h2p.shape
    E = w_router.shape[1]
    tm = _pick(M, (256, 128, 64, 32, 16, 8))
    out = pl.BlockSpec((tm, LANES), lambda i: (i, 0))
    return pl.pallas_call(
        functools.partial(_router_kernel, n_experts=E),
        grid=(M // tm,),
        in_specs=[pl.BlockSpec((tm, H), lambda i: (i, 0)),
                  pl.BlockSpec((H, E), lambda i: (0, 0)),
                  pl.BlockSpec((H, E), lambda i: (1, 0)),
                  pl.BlockSpec((1, E), lambda i: (0, 0))],
        out_specs=[out, out, out, pl.BlockSpec((1, E), lambda i: (0, 0))],
        out_shape=[jax.ShapeDtypeStruct((M, LANES), jnp.int32), jax.ShapeDtypeStruct((M, LANES), F32),
                   jax.ShapeDtypeStruct((M, LANES), jnp.int32), jax.ShapeDtypeStruct((1, E), F32)],
        scratch_shapes=[pltpu.VMEM((1, E), F32)],
        compiler_params=_cparams("arbitrary"),
        name="router",
    )(h2p, w_router, w_router, bias)


def _row_copy(src, src_row, dst, dst_row, sem):
    return pltpu.make_async_copy(src.at[pl.ds(src_row, 1)], dst.at[pl.ds(dst_row, 1)], sem)


def _pos_copy(pos_hbm, pos_smem, isem, tile, s, n_rows):
    start = pl.multiple_of(s * n_rows, n_rows)
    return pltpu.make_async_copy(pos_hbm.at[tile], pos_smem.at[pl.ds(start, n_rows)], isem.at[s])


def _pad_chunks(rows):
    n, out = rows // 2, []
    while n >= SUBLANES:
        out.append(n)
        n //= 2
    return out


def _dispatch_kernel(pstart_ref, count_ref, padded_ref, nused_ref, pos_hbm, h_ref, xs_hbm,
                     pos_smem, xbuf, zbuf, isem, dsem, zsem, *, tm, n_experts, bm):
    i = pl.program_id(0)
    nt = pl.num_programs(0)
    slot = i % 2
    nxt = 1 - slot
    n_rows = tm * TOP_K

    def pad_fill(wait):
        def body(e, c):
            npad = padded_ref[e] - count_ref[e]
            off = pstart_ref[e] + count_ref[e]
            n_single = (-count_ref[e]) & (SUBLANES - 1)
            for j in range(SUBLANES - 1):
                cp = _row_copy(zbuf, 0, xs_hbm, off + j, zsem)

                @pl.when(j < n_single)
                def _():
                    cp.wait() if wait else cp.start()

            off = pl.multiple_of(off + n_single, SUBLANES)
            n_tiled = npad - n_single
            for n in _pad_chunks(bm):
                cp = pltpu.make_async_copy(zbuf.at[pl.ds(0, n)], xs_hbm.at[pl.ds(off, n)], zsem)

                @pl.when((n_tiled & n) != 0)
                def _():
                    cp.wait() if wait else cp.start()

                off = pl.multiple_of(off + (n_tiled & n), SUBLANES)
            return c
        lax.fori_loop(0, n_experts, body, 0)

        def tail(b, c):
            for part in range(2):
                row0 = pl.multiple_of(b * bm + part * (bm // 2), SUBLANES)
                cp = pltpu.make_async_copy(zbuf, xs_hbm.at[pl.ds(row0, bm // 2)], zsem)
                cp.wait() if wait else cp.start()
            return c
        lax.fori_loop(nused_ref[0], xs_hbm.shape[0] // bm, tail, 0)

    @pl.when(i == 0)
    def _():
        zbuf[...] = jnp.zeros_like(zbuf)
        pad_fill(False)
        _pos_copy(pos_hbm, pos_smem, isem, 0, 0, n_rows).start()
        _pos_copy(pos_hbm, pos_smem, isem, 0, 0, n_rows).wait()
        pad_fill(True)

    @pl.when(i + 1 < nt)
    def _():
        _pos_copy(pos_hbm, pos_smem, isem, i + 1, nxt, n_rows).start()

    xbuf[slot] = h_ref[...]
    base = slot * n_rows

    for r in range(tm):
        for k in range(TOP_K):
            _row_copy(xbuf.at[slot], r, xs_hbm, pos_smem[base + r * TOP_K + k], dsem.at[slot]).start()

    def wait_all(s):
        def body(r, c):
            _row_copy(xbuf.at[s], 0, xs_hbm, 0, dsem.at[s]).wait()
            return c
        lax.fori_loop(0, n_rows, body, 0, unroll=8)

    @pl.when(i > 0)
    def _():
        wait_all(nxt)

    @pl.when(i + 1 < nt)
    def _():
        _pos_copy(pos_hbm, pos_smem, isem, i + 1, nxt, n_rows).wait()

    @pl.when(i == nt - 1)
    def _():
        wait_all(slot)


def dispatch_tokens(h2p, pos_tiles, pstart, counts, padded, n_used, n_slots, bm):
    M, H = h2p.shape
    nt, n_rows = pos_tiles.shape
    tm = n_rows // TOP_K
    any_spec = pl.BlockSpec(memory_space=pl.ANY)
    grid_spec = pltpu.PrefetchScalarGridSpec(
        num_scalar_prefetch=4,
        grid=(nt,),
        in_specs=[any_spec, pl.BlockSpec((tm, H), lambda i, *_: (i, 0))],
        out_specs=any_spec,
        scratch_shapes=[pltpu.SMEM((2 * n_rows,), jnp.int32), pltpu.VMEM((2, tm, H), jnp.uint32),
                        pltpu.VMEM((bm // 2, H), jnp.uint32),
                        pltpu.SemaphoreType.DMA((2,)), pltpu.SemaphoreType.DMA((2,)),
                        pltpu.SemaphoreType.DMA],
    )
    return pl.pallas_call(
        functools.partial(_dispatch_kernel, tm=tm, n_experts=pstart.shape[0], bm=bm),
        grid_spec=grid_spec,
        out_shape=jax.ShapeDtypeStruct((n_slots, H), jnp.uint32),
        compiler_params=_cparams("arbitrary"),
        name="dispatch",
    )(pstart, counts, padded, n_used, pos_tiles, h2p)


def _experts_kernel(blk_e_ref, first_ref, next_ref, short_ref, nused_ref, xs_ref, w1_hbm, w3_hbm, w2_hbm, y_ref,
                    st1, st3, st2, wb1, wb3, wb2, wsem):
    i = pl.program_id(0)
    half = xs_ref.shape[1]

    mats = ((w1_hbm, st1, wb1), (w3_hbm, st3, wb3), (w2_hbm, st2, wb2))

    def weight_copy(m, e):
        return pltpu.make_async_copy(mats[m][0].at[e], mats[m][1], wsem.at[m])

    @pl.when(i == 0)
    def _():
        for m in range(3):
            weight_copy(m, blk_e_ref[0]).start()

    @pl.when(first_ref[i] == 1)
    def _():
        for m in range(3):
            weight_copy(m, blk_e_ref[i]).wait()
        nxt = jnp.maximum(next_ref[i], 0)
        for m, (_, st, wb) in enumerate(mats):
            wb[...] = st[...].astype(BF16)

            @pl.when(next_ref[i] >= 0)
            def _():
                weight_copy(m, nxt).start()

    def swiglu(x):
        x_lo, x_hi = _unpack_pair(x)
        x_lo = x_lo.astype(BF16)
        x_hi = x_hi.astype(BF16)
        h1 = _dot(x_lo, wb1[0:half, :]) + _dot(x_hi, wb1[half:2 * half, :])
        h3 = _dot(x_lo, wb3[0:half, :]) + _dot(x_hi, wb3[half:2 * half, :])
        hh = (h1 * _sigmoid(h1) * h3).astype(BF16)
        return _pack_pair(_dot(hh, wb2[...]))

    @pl.when((i < nused_ref[0]) & (short_ref[i] == 0))
    def _():
        y_ref[...] = swiglu(xs_ref[...])

    @pl.when((i < nused_ref[0]) & (short_ref[i] == 1))
    def _():
        hb = xs_ref.shape[0] // 2
        y_ref[0:hb, :] = swiglu(xs_ref[0:hb, :])
        y_ref[hb:2 * hb, :] = jnp.zeros((hb, half), y_ref.dtype)

    @pl.when(i >= nused_ref[0])
    def _():
        y_ref[...] = jnp.zeros_like(y_ref)


def routed_expert_blocks(xs, blk_e, first, next_e, short, n_used, w1, w3, w2, bm):
    n_slots, H = xs.shape
    nb = n_slots // bm
    _, D, F = w1.shape
    any_spec = pl.BlockSpec(memory_space=pl.ANY)
    grid_spec = pltpu.PrefetchScalarGridSpec(
        num_scalar_prefetch=5,
        grid=(nb,),
        in_specs=[pl.BlockSpec((bm, H), lambda i, be, fi, ne, sh, nu: (jnp.minimum(i, nu[0] - 1), 0)),
                  any_spec, any_spec, any_spec],
        out_specs=pl.BlockSpec((bm, H), lambda i, be, fi, ne, sh, nu: (i, 0)),
        scratch_shapes=[pltpu.VMEM((D, F), F32), pltpu.VMEM((D, F), F32), pltpu.VMEM((F, D), F32),
                        pltpu.VMEM((D, F), BF16), pltpu.VMEM((D, F), BF16), pltpu.VMEM((F, D), BF16),
                        pltpu.SemaphoreType.DMA((3,))],
    )
    return pl.pallas_call(
        _experts_kernel,
        grid_spec=grid_spec,
        out_shape=jax.ShapeDtypeStruct((n_slots, H), jnp.uint32),
        compiler_params=_cparams("arbitrary"),
        name="routed_experts",
    )(blk_e, first, next_e, short, n_used, xs, w1, w3, w2)


def _final_kernel(pos_hbm, y_hbm, x1_ref, h2_ref, wt_ref, g2_ref, ws1_ref, ws3_ref, ws2_ref,
                  w_ref, b_ref, o_ref, ybuf, pos_smem, gsem, isem, *, tm, alpha, tile_off):
    g = pl.program_id(0)
    i = pl.program_id(1)
    nt = pl.num_programs(1)
    step = g * nt + i
    nsteps = pl.num_programs(0) * nt
    slot = step % 2
    nxt = 1 - slot
    n_rows = tm * TOP_K
    half = h2_ref.shape[1]

    def pos_copy(st, s):
        return _pos_copy(pos_hbm, pos_smem, isem, tile_off + st, s, n_rows)

    def issue_rows(s):
        base = s * n_rows
        for r in range(tm):
            for k in range(TOP_K):
                _row_copy(y_hbm, pos_smem[base + r * TOP_K + k], ybuf.at[s], k * tm + r, gsem.at[s]).start()

    def wait_rows(s):
        def body(r, c):
            _row_copy(y_hbm, 0, ybuf.at[s], r, gsem.at[s]).wait()
            return c
        lax.fori_loop(0, n_rows, body, 0, unroll=8)

    @pl.when(step == 0)
    def _():
        pos_copy(0, 0).start()
        pos_copy(0, 0).wait()
        issue_rows(0)

        @pl.when(nsteps > 1)
        def _():
            pos_copy(1, 1).start()

    @pl.when(step + 1 < nsteps)
    def _():
        pos_copy(step + 1, nxt).wait()

    issue_rows(nxt)

    h_lo, h_hi = _unpack_pair(h2_ref[...])
    h_lo = h_lo.astype(BF16)
    h_hi = h_hi.astype(BF16)
    a1 = _dot(h_lo, ws1_ref[0:half, :]) + _dot(h_hi, ws1_ref[half:2 * half, :])
    a3 = _dot(h_lo, ws3_ref[0:half, :]) + _dot(h_hi, ws3_ref[half:2 * half, :])
    shared = _dot((a1 * _sigmoid(a1) * a3).astype(BF16), ws2_ref[...])
    ffn_lo = shared[:, :half]
    ffn_hi = shared[:, half:]

    wait_rows(slot)

    @pl.when(step + 2 < nsteps)
    def _():
        pos_copy(step + 2, slot).start()

    wt = wt_ref[...]
    for k in range(TOP_K):
        y_lo, y_hi = _unpack_pair(ybuf[slot, k * tm:(k + 1) * tm, :])
        wk = wt[:, k:k + 1]
        ffn_lo = ffn_lo + wk * y_lo
        ffn_hi = ffn_hi + wk * y_hi
    ffn = jnp.concatenate([ffn_lo, ffn_hi], axis=-1)
    o_ref[...] = _ln(alpha * x1_ref[...] + g2_ref[...] * ffn) * w_ref[...] + b_ref[...]

    @pl.when(step == nsteps - 1)
    def _():
        wait_rows(nxt)


def combine_and_finish(x1, h2p, wts, pos_tiles, tile_off, y, g2, ws1, ws3, ws2, ln_w, ln_b, alpha):
    G, R, D = x1.shape
    H = D // 2
    n_rows = pos_tiles.shape[1]
    tm = n_rows // TOP_K
    F = ws1.shape[1]
    assert G * (R // tm) >= 2, "the row prefetch needs at least two token tiles"
    any_spec = pl.BlockSpec(memory_space=pl.ANY)
    row = lambda w: pl.BlockSpec((None, tm, w), lambda g, i: (g, i, 0))
    const = lambda r, c: pl.BlockSpec((r, c), lambda g, i: (0, 0))
    kern = functools.partial(_final_kernel, tm=tm, alpha=alpha, tile_off=tile_off)
    return pl.pallas_call(
        kern,
        grid=(G, R // tm),
        in_specs=[any_spec, any_spec, row(D), row(H), row(LANES), _mod_spec(g2, tm),
                  const(D, F), const(D, F), const(F, D), const(1, D), const(1, D)],
        out_specs=row(D),
        out_shape=jax.ShapeDtypeStruct((G, R, D), F32),
        scratch_shapes=[pltpu.VMEM((2, n_rows, H), jnp.uint32), pltpu.SMEM((2 * n_rows,), jnp.int32),
                        pltpu.SemaphoreType.DMA((2,)), pltpu.SemaphoreType.DMA((2,))],
        compiler_params=_cparams("arbitrary", "arbitrary"),
        name="combine_finish",
    )(pos_tiles, y, x1, h2p, wts, g2, ws1, ws3, ws2, ln_w, ln_b)


def _slots_kernel(idx_ref, rank_ref, pstart_ref, pos_ref):
    tm = idx_ref.shape[0]
    E = pstart_ref.shape[1]
    lane = lax.broadcasted_iota(jnp.int32, (tm, E), 1)
    lane_o = lax.broadcasted_iota(jnp.int32, (tm, LANES), 1)
    idx = idx_ref[...]
    ps = pstart_ref[...]
    start = jnp.zeros((tm, LANES), F32)
    for k in range(TOP_K):
        sk = jnp.sum(jnp.where(lane == idx[:, k:k + 1], ps, 0.0), axis=-1, keepdims=True)
        start = jnp.where(lane_o == k, sk, start)
    pos_ref[...] = start.astype(jnp.int32) + rank_ref[...]


def assignment_slots(idx, rank, pstart):
    M = idx.shape[0]
    E = pstart.shape[1]
    tm = _pick(M, (256, 128, 64, 32, 16, 8))
    blk = pl.BlockSpec((tm, LANES), lambda i: (i, 0))
    return pl.pallas_call(
        _slots_kernel,
        grid=(M // tm,),
        in_specs=[blk, blk, pl.BlockSpec((1, E), lambda i: (0, 0))],
        out_specs=blk,
        out_shape=jax.ShapeDtypeStruct((M, LANES), jnp.int32),
        compiler_params=_cparams("parallel"),
        name="assignment_slots",
    )(idx, rank, pstart)


def _block_tables(counts, n_blocks, bm):
    E = counts.shape[0]
    padded = (counts + bm - 1) // bm * bm
    pend = jnp.cumsum(padded)
    pstart = pend - padded
    n_used = pend[-1] // bm
    blk = jnp.arange(n_blocks, dtype=jnp.int32)
    blk_e = jnp.minimum(jnp.sum((pend[None, :] <= (blk * bm)[:, None]).astype(jnp.int32), axis=1), E - 1)
    prev_e = jnp.concatenate([jnp.full((1,), -1, jnp.int32), blk_e[:-1]])
    first = ((blk_e != prev_e) & (blk < n_used)).astype(jnp.int32)
    ids = jnp.arange(E, dtype=jnp.int32)
    active = jnp.where(counts > 0, ids, E)
    later = lax.cummin(active[::-1])[::-1]
    nxt = jnp.concatenate([later[1:], jnp.full((1,), E, jnp.int32)])
    nxt = jnp.where(nxt >= E, -1, nxt)
    mine = ids[None, :] == blk_e[:, None]
    next_e = jnp.sum(jnp.where(mine, nxt[None, :], 0), axis=1).astype(jnp.int32)
    group_end = jnp.sum(jnp.where(mine, (pstart + counts)[None, :], 0), axis=1)
    rows = jnp.clip(group_end - blk * bm, 0, bm)
    short = ((rows <= bm // 2) & (blk < n_used)).astype(jnp.int32)
    i32 = lambda v: v.astype(jnp.int32)
    return i32(pstart), i32(padded), blk_e, first, next_e, short, i32(n_used).reshape(1)


def _assignment_tiles(v, tm):
    M = v.shape[0]
    return v[:, :TOP_K].reshape(M // tm, tm * TOP_K)


def _stream_mixers(x, mods, pos, s_ret, h_rnn, conv_buf, lw, alpha):
    B, T, D = x.shape
    n_heads, d_ret = lw["n_heads"], lw["d_ret"]
    d_rnn = lw["d_rnn"]
    per_row = mods["per_row"]
    xg = x.reshape(1, B * T, D) if per_row else x
    h = ln_mod(xg, mods["sc1"], mods["sh1"]).reshape(B * T, D)
    proj = matmul(h, lw["w_in"]).reshape(B, T, -1)
    pre_ret, s_new = retention_mixer(proj, pos, s_ret, lw["ret_gn_w"], n_heads, d_ret)
    x_group = (4 * d_ret) // d_rnn
    pre_rnn, h_new, conv_new = rglru_mixer(proj, x_group, h_rnn[:, None, :], conv_buf, lw["conv_w"],
                                           lw["conv_b"], lw["w_a"], lw["b_a"], lw["w_x"], lw["b_x"],
                                           lw["rg_lambda"])
    mixin = merge_branches(pre_ret.reshape(B * T, d_ret), pre_rnn.reshape(B * T, d_rnn),
                           lw["w_br_ret"], lw["w_br_rnn"], proj.reshape(B * T, -1), 4 * d_ret + 2 * d_rnn)
    mix = matmul(mixin, lw["w_o"])
    x1, h2 = post_mixer(xg, mix.reshape(xg.shape), mods["g1"], mods["sc2"], mods["sh2"],
                        lw["ln1_w"], lw["ln1_b"], alpha)
    return x1, h2, s_new, h_new[:, 0, :], conv_new


def _split_mods(mod, B, T, per_row):
    D = mod.shape[1] // 6
    parts = mod.reshape(B, 6, D)
    out = {"per_row": per_row}
    for n, name in enumerate(("sh1", "sc1", "g1", "sh2", "sc2", "g2")):
        p = parts[:, n, :]
        out[name] = jnp.repeat(p, T, axis=0)[None] if per_row else p[:, None, :]
    return out


def kernel(x_prompt, x_sample, state_ret, state_rglru, state_conv, c_prompt, c_sample, w_ada, b_ada, w_in, ret_gn_w, conv_w, conv_b, w_a, b_a, w_x, b_x, rg_lambda, w_br_ret, w_br_rnn, w_o, ln1_w, ln1_b, w_router, router_bias, w1, w3, w2, ws1, ws3, ws2, ln2_w, ln2_b):
    B, T, D = x_prompt.shape
    Bs, Ts, _ = x_sample.shape
    depth = w_ada.shape[0]
    n_heads, dk, dv = state_ret.shape[2:]
    d_ret = n_heads * dk
    d_rnn = state_rglru.shape[-1]
    cw = conv_w.shape[1]
    n_experts = w_router.shape[-1]
    alpha = (2.0 * depth) ** 0.25
    pos_p = jnp.arange(T, dtype=jnp.int32)
    pos_s = PAST_LEN + jnp.arange(Ts, dtype=jnp.int32)
    row = lambda v: v.reshape(1, -1)

    yp, ys = x_prompt, x_sample
    outs = [[] for _ in range(6)]
    for l in range(depth):
        lw = dict(n_heads=n_heads, d_ret=d_ret, d_rnn=d_rnn,
                  w_in=w_in[l].astype(BF16), ret_gn_w=row(ret_gn_w[l]), conv_w=conv_w[l], conv_b=row(conv_b[l]),
                  w_a=w_a[l].astype(BF16), b_a=row(b_a[l]), w_x=w_x[l].astype(BF16), b_x=row(b_x[l]),
                  rg_lambda=row(rg_lambda[l]), w_br_ret=w_br_ret[l].astype(BF16),
                  w_br_rnn=w_br_rnn[l].astype(BF16), w_o=w_o[l].astype(BF16),
                  ln1_w=row(ln1_w[l]), ln1_b=row(ln1_b[l]))
        mod = ada_mod(jnp.concatenate([c_prompt, c_sample], axis=0), w_ada[l], row(b_ada[l]))
        mods_p = _split_mods(mod[:B], B, T, per_row=False)
        mods_s = _split_mods(mod[B:], Bs, Ts, per_row=True)

        zero_ret = jnp.zeros((B, n_heads, dk, dv), F32)
        zero_h = jnp.zeros((B, d_rnn), F32)
        zero_conv = jnp.zeros((B, cw - 1, d_rnn), F32)
        x1p, h2p, r1, h1, c1 = _stream_mixers(yp, mods_p, pos_p, zero_ret, zero_h, zero_conv, lw, alpha)
        x1s, h2s, r2, hh2, c2 = _stream_mixers(ys, mods_s, pos_s, state_ret[l], state_rglru[l],
                                               state_conv[l], lw, alpha)

        Mp, Ms = B * T, Bs * Ts
        M = Mp + Ms
        bm = EXPERT_ROWS
        tm = math.gcd(math.gcd(T, Ms // 2), TOKEN_TILE)
        h2_all = jnp.concatenate([h2p.reshape(Mp, D // 2), h2s.reshape(Ms, D // 2)], axis=0)
        idx_l, wt_l, rank_l, cnt = router(h2_all, w_router[l].astype(BF16), row(router_bias[l]))
        n_blocks = -(-(M * TOP_K) // bm) + n_experts
        pstart, padded, blk_e, first, next_e, short, n_used = _block_tables(cnt[0].astype(jnp.int32), n_blocks, bm)
        pos_t = _assignment_tiles(assignment_slots(idx_l, rank_l, row(pstart).astype(F32)), tm)
        xs = dispatch_tokens(h2_all, pos_t, pstart, cnt[0].astype(jnp.int32), padded, n_used, n_blocks * bm, bm)
        y_slots = routed_expert_blocks(xs, blk_e, first, next_e, short, n_used, w1[l], w3[l], w2[l], bm)
        ws = (ws1[l].astype(BF16), ws3[l].astype(BF16), ws2[l].astype(BF16))
        yp = combine_and_finish(x1p, h2p, wt_l[:Mp].reshape(B, T, LANES), pos_t, 0,
                                y_slots, mods_p["g2"], *ws, row(ln2_w[l]), row(ln2_b[l]), alpha)
        ys = combine_and_finish(x1s, h2s, wt_l[Mp:].reshape(1, Ms, LANES), pos_t, Mp // tm,
                                y_slots, mods_s["g2"], *ws, row(ln2_w[l]), row(ln2_b[l]), alpha)
        ys = ys.reshape(Bs, Ts, D)
        for lst, val in zip(outs, (r1, h1, c1, r2, hh2, c2)):
            lst.append(val)
    return (yp, ys) + tuple(jnp.stack(o) for o in outs)
```
